```python
import math
import jax, jax.numpy as jnp
from jax import lax
import numpy as np

D_MODEL = 1024
BATCH = 8
SEQ = 4096
DEPTH = 2

N_MIXERS = 2
EXPAND = 2
BRANCH = EXPAND * D_MODEL
EPS = 1e-6

DA_HEAD = 64
DA_HEADS = BRANCH // (2 * DA_HEAD)
DA_VDIM = 2 * DA_HEAD
Q_BLOCK = 128

S5_GROUP = 16
S5_GROUPS = BRANCH // S5_GROUP
S5_STATE = 64
S5_CHUNK = 128
DT_MIN = 1e-3
DT_MAX = 1e-1

kernel_name = "hybrid_diffattn_s5_gated"


def rms_norm(x, g):
    xf = x.astype(jnp.float32)
    y = xf * lax.rsqrt(jnp.mean(xf * xf, axis=-1, keepdims=True) + EPS)
    return (y * g.astype(jnp.float32)).astype(x.dtype)


def alibi_slopes(n_heads):
    return jnp.asarray(2.0 ** (-8.0 * np.arange(1, n_heads + 1) / n_heads), dtype=jnp.float32)


def lambda_init(layer_idx):
    return 0.8 - 0.6 * math.exp(-0.3 * layer_idx)


def diff_attention_layer(x, norm_g, w_in, q_norm_g, k_norm_g, lam_q1, lam_k1,
                         lam_q2, lam_k2, head_norm_g, w_out, layer_idx):
    b, s, _ = x.shape
    h = rms_norm(x, norm_g)
    q, k, v, z = jnp.split(h @ w_in, 4, axis=-1)
    q = rms_norm(q.reshape(b, s, DA_HEADS, 2, DA_HEAD), q_norm_g) * (DA_HEAD ** -0.5)
    k = rms_norm(k.reshape(b, s, DA_HEADS, 2, DA_HEAD), k_norm_g)
    v = v.reshape(b, s, DA_HEADS, DA_VDIM)

    f32 = jnp.float32
    lam_0 = lambda_init(layer_idx)
    lam = (jnp.exp(jnp.sum(lam_q1.astype(f32) * lam_k1.astype(f32)))
           - jnp.exp(jnp.sum(lam_q2.astype(f32) * lam_k2.astype(f32))) + lam_0)
    slopes = alibi_slopes(DA_HEADS)

    outs = []
    for blk in range(s // Q_BLOCK):
        q0 = blk * Q_BLOCK
        kv_len = q0 + Q_BLOCK
        qb = q[:, q0:kv_len]
        kb = k[:, :kv_len]
        vb = v[:, :kv_len]
        scores = jnp.einsum('bqhmd,bkhmd->bhmqk', qb, kb,
                            preferred_element_type=f32)
        dist = (jnp.arange(q0, kv_len)[:, None] - jnp.arange(kv_len)[None, :]).astype(f32)
        bias = jnp.where(dist >= 0, -slopes[:, None, None] * dist, -jnp.inf)
        probs = jax.nn.softmax(scores + bias[None, :, None], axis=-1)
        weights = probs[:, :, 0] - lam * probs[:, :, 1]
        outs.append(jnp.einsum('bhqk,bkhe->bqhe', weights.astype(v.dtype), vb))
    o = jnp.concatenate(outs, axis=1)
    o = rms_norm(o, head_norm_g) * (1.0 - lam_0)
    o = o.reshape(b, s, BRANCH) * jax.nn.silu(z)
    return x + o @ w_out


def _ssm_combine(e1, e2):
    a1r, a1i, b1r, b1i = e1
    a2r, a2i, b2r, b2i = e2
    return (a2r * a1r - a2i * a1i,
            a2r * a1i + a2i * a1r,
            a2r * b1r - a2i * b1i + b2r,
            a2r * b1i + a2i * b1r + b2i)


def s5_layer(x, norm_g, w_in, lam_re, lam_im, log_dt, b_re, b_im, c_re, c_im,
             d_skip, w_glu, b_glu, w_out):
    b, s, _ = x.shape
    f32 = jnp.float32
    h = rms_norm(x, norm_g)
    u, z = jnp.split(h @ w_in, 2, axis=-1)

    dt = jnp.exp(log_dt.astype(f32))[:, None]
    lr, li = lam_re.astype(f32), lam_im.astype(f32)
    mag = jnp.exp(lr * dt)
    ab_re, ab_im = mag * jnp.cos(li * dt), mag * jnp.sin(li * dt)
    den = lr * lr + li * li
    nr, ni = ab_re - 1.0, ab_im
    g_re = (nr * lr + ni * li) / den
    g_im = (ni * lr - nr * li) / den
    br, bi = b_re.astype(f32), b_im.astype(f32)
    bb_re = g_re[..., None] * br - g_im[..., None] * bi
    bb_im = g_re[..., None] * bi + g_im[..., None] * br
    cr, ci = c_re.astype(f32), c_im.astype(f32)

    n_chunks = s // S5_CHUNK
    ug = u.astype(f32).reshape(b, n_chunks, S5_CHUNK, S5_GROUPS, S5_GROUP)
    ug = jnp.moveaxis(ug, 1, 0)
    a_shape = (b, S5_CHUNK, S5_GROUPS, S5_STATE)
    a_re = jnp.broadcast_to(ab_re, a_shape)
    a_im = jnp.broadcast_to(ab_im, a_shape)

    def chunk_step(carry, u_c):
        h_re, h_im = carry
        bu_re = jnp.einsum('blgc,gpc->blgp', u_c, bb_re)
        bu_im = jnp.einsum('blgc,gpc->blgp', u_c, bb_im)
        acc_re, acc_im, st_re, st_im = lax.associative_scan(
            _ssm_combine, (a_re, a_im, bu_re, bu_im), axis=1)
        st_re = acc_re * h_re[:, None] - acc_im * h_im[:, None] + st_re
        st_im = acc_re * h_im[:, None] + acc_im * h_re[:, None] + st_im
        y = (jnp.einsum('blgp,gcp->blgc', st_re, cr)
             - jnp.einsum('blgp,gcp->blgc', st_im, ci))
        return (st_re[:, -1], st_im[:, -1]), y

    h0 = jnp.zeros((b, S5_GROUPS, S5_STATE), f32)
    _, ys = lax.scan(chunk_step, (h0, h0), ug)
    y = jnp.moveaxis(ys, 0, 1).reshape(b, s, BRANCH)
    y = y + d_skip.astype(f32) * u.astype(f32)
    y = jax.nn.gelu(y)
    y = y * jax.nn.sigmoid(y @ w_glu.astype(f32) + b_glu.astype(f32))
    y = y.astype(x.dtype) * jax.nn.silu(z)
    return x + y @ w_out


def setup_inputs(seed: int = 0) -> dict:
    key = jax.random.key(seed)
    ks = jax.random.split(key, 32)
    nrm = jax.random.normal
    f32 = jnp.float32
    D, E, G, P, C = D_MODEL, BRANCH, S5_GROUPS, S5_STATE, S5_GROUP
    inp = {}
    inp['x'] = nrm(ks[0], (BATCH, SEQ, D), f32)
    inp['l0_norm_g'] = 1.0 + 0.02 * nrm(ks[1], (D,), f32)
    inp['l0_w_in'] = nrm(ks[2], (D, 4 * E), f32) * D ** -0.5
    inp['l0_q_norm_g'] = 1.0 + 0.02 * nrm(ks[3], (DA_HEAD,), f32)
    inp['l0_k_norm_g'] = 1.0 + 0.02 * nrm(ks[4], (DA_HEAD,), f32)
    inp['l0_lam_q1'] = 0.1 * nrm(ks[5], (DA_HEAD,), f32)
    inp['l0_lam_k1'] = 0.1 * nrm(ks[6], (DA_HEAD,), f32)
    inp['l0_lam_q2'] = 0.1 * nrm(ks[7], (DA_HEAD,), f32)
    inp['l0_lam_k2'] = 0.1 * nrm(ks[8], (DA_HEAD,), f32)
    inp['l0_head_norm_g'] = 1.0 + 0.02 * nrm(ks[9], (DA_VDIM,), f32)
    inp['l0_w_out'] = nrm(ks[10], (E, D), f32) * E ** -0.5
    inp['l1_norm_g'] = 1.0 + 0.02 * nrm(ks[11], (D,), f32)
    inp['l1_w_in'] = nrm(ks[12], (D, 2 * E), f32) * D ** -0.5
    n = jnp.arange(P, dtype=f32)
    inp['l1_lam_re'] = -0.5 + 0.01 * nrm(ks[13], (G, P), f32)
    inp['l1_lam_im'] = jnp.pi * n[None, :] + 0.01 * nrm(ks[14], (G, P), f32)
    inp['l1_log_dt'] = jax.random.uniform(ks[15], (G,), f32, math.log(DT_MIN), math.log(DT_MAX))
    inp['l1_b_re'] = nrm(ks[16], (G, P, C), f32) * (2.0 * C) ** -0.5
    inp['l1_b_im'] = nrm(ks[17], (G, P, C), f32) * (2.0 * C) ** -0.5
    inp['l1_c_re'] = nrm(ks[18], (G, C, P), f32) * (2.0 * P) ** -0.5
    inp['l1_c_im'] = nrm(ks[19], (G, C, P), f32) * (2.0 * P) ** -0.5
    inp['l1_d'] = 1.0 + 0.1 * nrm(ks[20], (E,), f32)
    inp['l1_w_glu'] = nrm(ks[21], (E, E), f32) * E ** -0.5
    inp['l1_b_glu'] = 0.01 * nrm(ks[22], (E,), f32)
    inp['l1_w_out'] = nrm(ks[23], (E, D), f32) * E ** -0.5
    return inp


def reference(x, l0_norm_g, l0_w_in, l0_q_norm_g, l0_k_norm_g, l0_lam_q1, l0_lam_k1,
              l0_lam_q2, l0_lam_k2, l0_head_norm_g, l0_w_out,
              l1_norm_g, l1_w_in, l1_lam_re, l1_lam_im, l1_log_dt, l1_b_re, l1_b_im,
              l1_c_re, l1_c_im, l1_d, l1_w_glu, l1_b_glu, l1_w_out):
    mixers = [
        lambda h: diff_attention_layer(h, l0_norm_g, l0_w_in, l0_q_norm_g, l0_k_norm_g,
                                       l0_lam_q1, l0_lam_k1, l0_lam_q2, l0_lam_k2,
                                       l0_head_norm_g, l0_w_out, 0),
        lambda h: s5_layer(h, l1_norm_g, l1_w_in, l1_lam_re, l1_lam_im, l1_log_dt,
                           l1_b_re, l1_b_im, l1_c_re, l1_c_im, l1_d, l1_w_glu,
                           l1_b_glu, l1_w_out),
    ]
    for i in range(DEPTH):
        x = mixers[i % N_MIXERS](x)
    return x
```

```python
import functools
import math

import jax
import jax.numpy as jnp
import numpy as np
from jax import lax
from jax.experimental import pallas as pl
from jax.experimental.pallas import tpu as pltpu

F32 = jnp.float32
BF16 = jnp.bfloat16

EPS = 1e-6
SUB_HEAD = 64
HEAD_V = 2 * SUB_HEAD
GROUP_CH = 16
GROUP_STATE = 64
LAMBDA_INIT_L0 = 0.8 - 0.6 * math.exp(-0.3 * 0)

LANES = 128
SUBLANES = 8
MXU_DIM = 256
VMEM_LIMIT_BYTES = 56 * 1024 * 1024
NEG_BIG = -1e30

SSM_TILE_CH = MXU_DIM
SSM_TILE_GROUPS = SSM_TILE_CH // GROUP_CH
SSM_TILE_STATE = SSM_TILE_GROUPS * GROUP_STATE
SCAN_LANES = 512


def _params(*sem):
    return pltpu.CompilerParams(dimension_semantics=sem, vmem_limit_bytes=VMEM_LIMIT_BYTES)


def _const_spec(shape):
    nd = len(shape)
    return pl.BlockSpec(shape, lambda *_: (0,) * nd, pipeline_mode=pl.Buffered(1))


def _pick(n, pref):
    t = min(n, pref)
    while n % t:
        t //= 2
    return t


def _l0_inproj_kernel(x_ref, g_ref, w_ref, cs_ref, e_ref, o_ref, h_scr, *, n_norm_tiles):
    j = pl.program_id(1)

    @pl.when(j == 0)
    def _():
        xf = x_ref[...]
        ms = jnp.mean(xf * xf, axis=-1, keepdims=True)
        h_scr[...] = (xf * lax.rsqrt(ms + EPS) * g_ref[...]).astype(BF16)

    acc = jnp.dot(h_scr[...], w_ref[...], preferred_element_type=F32)
    tn = acc.shape[1]

    @pl.when(j < n_norm_tiles)
    def _():
        sq = acc * acc
        hi = sq.astype(BF16)
        lo = (sq - hi.astype(F32)).astype(BF16)
        e = e_ref[...]
        for c in range(tn // MXU_DIM):
            sl = slice(c * MXU_DIM, (c + 1) * MXU_DIM)
            ss = (jnp.dot(hi[:, sl], e, preferred_element_type=F32)
                  + jnp.dot(lo[:, sl], e, preferred_element_type=F32))
            inv = lax.rsqrt(ss * (1.0 / SUB_HEAD) + EPS)
            o_ref[:, sl] = (acc[:, sl] * inv * cs_ref[:, sl]).astype(BF16)

    @pl.when(j >= n_norm_tiles)
    def _():
        o_ref[...] = acc.astype(BF16)


def _l0_inproj(x2d, norm_g, w_bf, col_scale, n_norm_cols):
    m, d = x2d.shape
    n = w_bf.shape[1]
    tm = _pick(m, 1024)
    tn = _pick(n_norm_cols, 512)
    seg = np.arange(MXU_DIM) // SUB_HEAD
    e = jnp.asarray(seg[:, None] == seg[None, :], dtype=BF16)
    return pl.pallas_call(
        functools.partial(_l0_inproj_kernel, n_norm_tiles=n_norm_cols // tn),
        out_shape=jax.ShapeDtypeStruct((m, n), BF16),
        grid=(m // tm, n // tn),
        in_specs=[
            pl.BlockSpec((tm, d), lambda i, j: (i, 0)),
            _const_spec((1, d)),
            pl.BlockSpec((d, tn), lambda i, j: (0, j)),
            pl.BlockSpec((1, tn), lambda i, j: (0, j)),
            _const_spec((MXU_DIM, MXU_DIM)),
        ],
        out_specs=pl.BlockSpec((tm, tn), lambda i, j: (i, j)),
        scratch_shapes=[pltpu.VMEM((tm, d), BF16)],
        compiler_params=_params("parallel", "arbitrary"),
        name="l0_inproj",
    )(x2d, norm_g.reshape(1, d), w_bf, col_scale, e)


def _split3(val):
    p1 = val.astype(BF16).astype(F32)
    r1 = val - p1
    p2 = r1.astype(BF16).astype(F32)
    p3 = (r1 - p2).astype(BF16).astype(F32)
    return p1, p2, p3


def _augment(parts, lane, off, ones_first):
    p_off = off + 3 if ones_first else off
    o_off = off if ones_first else off + 3
    a = jnp.where(lane == p_off, parts[0], 0.0)
    a = jnp.where(lane == p_off + 1, parts[1], a)
    a = jnp.where(lane == p_off + 2, parts[2], a)
    for t in range(3):
        a = jnp.where(lane == o_off + t, 1.0, a)
    return a


def _attn_kernel(q_ref, k_ref, v_ref, z_ref, lam_ref, hg_ref, slope_ref, o_ref,
                 kaug_scr, acc_scr, m_scr, l_scr, *, tq):
    s_len = q_ref.shape[0]
    n_blk = s_len // tq
    h = pl.program_id(1)
    slope = slope_ref[pl.ds(h, 1), :]
    lane = lax.broadcasted_iota(jnp.int32, (tq, LANES), 1)
    row = lax.broadcasted_iota(jnp.int32, (tq, LANES), 0)
    low_half = lane < SUB_HEAD

    lam_v = lam_ref[...]
    lam = (jnp.exp(jnp.sum(lam_v[0:1] * lam_v[1:2], axis=1, keepdims=True))
           - jnp.exp(jnp.sum(lam_v[2:3] * lam_v[3:4], axis=1, keepdims=True)) + LAMBDA_INIT_L0)

    def build_k(c, carry):
        r0 = pl.multiple_of(c * tq, tq)
        kf = k_ref[pl.ds(r0, tq), :].astype(F32)
        parts = _split3(slope * (r0 + row).astype(F32))
        kaug_scr[0, pl.ds(r0, tq), :] = jnp.where(
            low_half, kf, _augment(parts, lane, SUB_HEAD, True)).astype(BF16)
        kaug_scr[1, pl.ds(r0, tq), :] = jnp.where(
            low_half, _augment(parts, lane, 0, True), kf).astype(BF16)
        return carry

    lax.fori_loop(0, n_blk, build_k, 0)

    col_s = lax.broadcasted_iota(jnp.int32, (tq, tq), 1)
    row_s = lax.broadcasted_iota(jnp.int32, (tq, tq), 0)
    causal = col_s <= row_s

    def q_block(i, carry):
        r0 = pl.multiple_of(i * tq, tq)
        qf = q_ref[pl.ds(r0, tq), :].astype(F32)
        parts = _split3(-slope * (r0 + row).astype(F32))
        q_aug = (jnp.where(low_half, qf, _augment(parts, lane, SUB_HEAD, False)).astype(BF16),
                 jnp.where(low_half, _augment(parts, lane, 0, False), qf).astype(BF16))
        m_scr[...] = jnp.full(m_scr.shape, NEG_BIG, F32)
        l_scr[...] = jnp.zeros(l_scr.shape, F32)
        acc_scr[...] = jnp.zeros(acc_scr.shape, F32)

        def kv_step(j, masked):
            c0 = pl.multiple_of(j * tq, tq)
            v = v_ref[pl.ds(c0, tq), :]
            for sub in range(2):
                kk = kaug_scr[sub, pl.ds(c0, tq), :]
                s = lax.dot_general(q_aug[sub], kk, (((1,), (1,)), ((), ())),
                                    preferred_element_type=F32)
                if masked:
                    s = jnp.where(causal, s, -jnp.inf)
                m_prev = m_scr[sub]
                m_new = jnp.maximum(m_prev, jnp.max(s, axis=1, keepdims=True))
                alpha = jnp.exp(m_prev - m_new)
                p = jnp.exp(s - pltpu.repeat(m_new, tq // LANES, axis=1))
                l_scr[sub] = alpha * l_scr[sub] + jnp.sum(p, axis=1, keepdims=True)
                acc_scr[sub] = alpha * acc_scr[sub] + jnp.dot(
                    p.astype(BF16), v, preferred_element_type=F32)
                m_scr[sub] = m_new

        def off_diag(j, c):
            kv_step(j, False)
            return c

        lax.fori_loop(0, i, off_diag, 0)
        kv_step(i, True)

        o = acc_scr[0] / l_scr[0] - lam * (acc_scr[1] / l_scr[1])
        ms = jnp.mean(o * o, axis=-1, keepdims=True)
        on = o * lax.rsqrt(ms + EPS) * hg_ref[...] * (1.0 - LAMBDA_INIT_L0)
        z = z_ref[pl.ds(r0, tq), :].astype(F32)
        o_ref[pl.ds(r0, tq), :] = (on * (z / (1.0 + jnp.exp(-z)))).astype(BF16)
        return carry

    lax.fori_loop(0, n_blk, q_block, 0)


def _l0_attn(qkvz, lam_vecs, head_g, n_heads):
    b, s, _ = qkvz.shape
    tq = _pick(s, 256)
    slopes = 2.0 ** (-8.0 * np.arange(1, n_heads + 1) / n_heads)
    slope_tab = jnp.asarray(np.broadcast_to(slopes[:, None], (n_heads, LANES)), dtype=F32)

    def head_block(which):
        return pl.BlockSpec((None, s, HEAD_V), lambda bi, hi: (bi, 0, which * n_heads + hi))

    return pl.pallas_call(
        functools.partial(_attn_kernel, tq=tq),
        out_shape=jax.ShapeDtypeStruct((b, s, n_heads * HEAD_V), BF16),
        grid=(b, n_heads),
        in_specs=[head_block(0), head_block(1), head_block(2), head_block(3),
                  _const_spec((4, SUB_HEAD)), _const_spec((1, HEAD_V)),
                  _const_spec((n_heads, LANES))],
        out_specs=pl.BlockSpec((None, s, HEAD_V), lambda bi, hi: (bi, 0, hi)),
        scratch_shapes=[pltpu.VMEM((2, s, HEAD_V), BF16),
                        pltpu.VMEM((2, tq, HEAD_V), F32),
                        pltpu.VMEM((2, tq, LANES), F32),
                        pltpu.VMEM((2, tq, LANES), F32)],
        compiler_params=_params("parallel", "parallel"),
        name="l0_attn",
    )(qkvz, qkvz, qkvz, qkvz, lam_vecs, head_g.reshape(1, HEAD_V), slope_tab)


def _l0_out_l1_in_kernel(o_ref, x_ref, wo_ref, g_ref, wi_ref, x1_ref, u_ref, z_ref):
    e = u_ref.shape[1]
    x1 = x_ref[...] + jnp.dot(o_ref[...], wo_ref[...], preferred_element_type=F32)
    x1_ref[...] = x1
    ms = jnp.mean(x1 * x1, axis=-1, keepdims=True)
    hn = (x1 * lax.rsqrt(ms + EPS) * g_ref[...]).astype(BF16)
    u_ref[...] = jnp.dot(hn, wi_ref[:, :e], preferred_element_type=F32).astype(BF16)
    z_ref[...] = jnp.dot(hn, wi_ref[:, e:], preferred_element_type=F32).astype(BF16)


def _l0_out_l1_in(o, x, wo_bf, norm_g, wi_bf):
    b, s, d = x.shape
    e = o.shape[2]
    tm = _pick(s, 512)
    tb_shape = jax.ShapeDtypeStruct((s, b * e), BF16)
    return pl.pallas_call(
        _l0_out_l1_in_kernel,
        out_shape=(jax.ShapeDtypeStruct((b, s, d), F32), tb_shape, tb_shape),
        grid=(b, s // tm),
        in_specs=[
            pl.BlockSpec((None, tm, e), lambda bi, si: (bi, si, 0)),
            pl.BlockSpec((None, tm, d), lambda bi, si: (bi, si, 0)),
            _const_spec((e, d)),
            _const_spec((1, d)),
            _const_spec((d, 2 * e)),
        ],
        out_specs=(
            pl.BlockSpec((None, tm, d), lambda bi, si: (bi, si, 0)),
            pl.BlockSpec((tm, e), lambda bi, si: (si, bi)),
            pl.BlockSpec((tm, e), lambda bi, si: (si, bi)),
        ),
        compiler_params=_params("parallel", "parallel"),
        name="l0_out_l1_in",
    )(o, x, wo_bf, norm_g.reshape(1, d), wi_bf)


def _ssm_kernel(u_ref, bt_ref, ct_ref, a_ref, y_ref, st_scr, h_scr, *, batch):
    c = pl.program_id(1)
    n_steps = u_ref.shape[0] // batch

    @pl.when(c == 0)
    def _():
        h_scr[...] = jnp.zeros(h_scr.shape, F32)

    st_scr[...] = jnp.dot(u_ref[...], bt_ref[...], preferred_element_type=F32)

    for w in range(SSM_TILE_STATE // SCAN_LANES):
        re = pl.ds(w * SCAN_LANES, SCAN_LANES)
        im = pl.ds(SSM_TILE_STATE + w * SCAN_LANES, SCAN_LANES)
        a_re = jnp.broadcast_to(a_ref[0:1, re], (batch, SCAN_LANES))
        a_im = jnp.broadcast_to(a_ref[1:2, re], (batch, SCAN_LANES))

        def step(t, carry):
            h_re, h_im = carry
            rows = pl.ds(pl.multiple_of(t * batch, batch), batch)
            n_re = a_re * h_re - a_im * h_im + st_scr[rows, re]
            n_im = a_re * h_im + a_im * h_re + st_scr[rows, im]
            st_scr[rows, re] = n_re
            st_scr[rows, im] = n_im
            return n_re, n_im

        h_re, h_im = lax.fori_loop(0, n_steps, step, (h_scr[:, re], h_scr[:, im]), unroll=4)
        h_scr[:, re] = h_re
        h_scr[:, im] = h_im

    y_ref[...] = jnp.dot(st_scr[...].astype(BF16), ct_ref[...],
                         preferred_element_type=F32).astype(BF16)


def _l1_ssm(u_tb, bt, ct, a_tiles, batch):
    rows, e = u_tb.shape
    n_tiles = e // SSM_TILE_CH
    steps = _pick(rows // batch, 128)
    blk = steps * batch
    return pl.pallas_call(
        functools.partial(_ssm_kernel, batch=batch),
        out_shape=jax.ShapeDtypeStruct((rows, e), BF16),
        grid=(n_tiles, rows // blk),
        in_specs=[
            pl.BlockSpec((blk, SSM_TILE_CH), lambda i, c: (c, i)),
            pl.BlockSpec((None, SSM_TILE_CH, 2 * SSM_TILE_STATE), lambda i, c: (i, 0, 0)),
            pl.BlockSpec((None, 2 * SSM_TILE_STATE, SSM_TILE_CH), lambda i, c: (i, 0, 0)),
            pl.BlockSpec((None, 2, SSM_TILE_STATE), lambda i, c: (i, 0, 0)),
        ],
        out_specs=pl.BlockSpec((blk, SSM_TILE_CH), lambda i, c: (c, i)),
        scratch_shapes=[pltpu.VMEM((blk, 2 * SSM_TILE_STATE), F32),
                        pltpu.VMEM((batch, 2 * SSM_TILE_STATE), F32)],
        compiler_params=_params("parallel", "arbitrary"),
        name="l1_ssm",
    )(u_tb, bt, ct, a_tiles)


def _glu_out_kernel(y_ref, u_ref, z_ref, x1_ref, d_ref, wg_ref, bg_ref, wo_ref, o_ref):
    y = y_ref[...].astype(F32) + d_ref[...] * u_ref[...].astype(F32)
    cdf = 0.5 * (1.0 + jnp.tanh(math.sqrt(2.0 / math.pi) * (y + 0.044715 * (y * y * y))))
    y = y * cdf
    gate = jnp.dot(y.astype(BF16), wg_ref[...], preferred_element_type=F32) + bg_ref[...]
    y = y * (1.0 / (1.0 + jnp.exp(-gate)))
    z = z_ref[...].astype(F32)
    y = y * (z / (1.0 + jnp.exp(-z)))
    o_ref[...] = x1_ref[...] + jnp.dot(y.astype(BF16), wo_ref[...], preferred_element_type=F32)


def _l1_glu_out(y_tb, u_tb, z_tb, x1, d_skip, wg_bf, b_glu, wo_bf):
    b, s, d = x1.shape
    e = wg_bf.shape[0]
    ts = _pick(s, 256)
    tb_spec = pl.BlockSpec((ts, e), lambda bi, si: (si, bi))
    return pl.pallas_call(
        _glu_out_kernel,
        out_shape=jax.ShapeDtypeStruct((b, s, d), F32),
        grid=(b, s // ts),
        in_specs=[tb_spec, tb_spec, tb_spec,
                  pl.BlockSpec((None, ts, d), lambda bi, si: (bi, si, 0)),
                  _const_spec((1, e)), _const_spec((e, e)), _const_spec((1, e)),
                  _const_spec((e, d))],
        out_specs=pl.BlockSpec((None, ts, d), lambda bi, si: (bi, si, 0)),
        compiler_params=_params("parallel", "parallel"),
        name="l1_glu_out",
    )(y_tb, u_tb, z_tb, x1, d_skip.reshape(1, e), wg_bf, b_glu.reshape(1, e), wo_bf)


def _s5_tiles(lam_re, lam_im, log_dt, b_re, b_im, c_re, c_im):
    g, p = lam_re.shape
    ch = b_re.shape[2]
    nt = g // SSM_TILE_GROUPS
    dt = jnp.exp(log_dt.astype(F32))[:, None]
    lr, li = lam_re.astype(F32), lam_im.astype(F32)
    mag = jnp.exp(lr * dt)
    ab_re, ab_im = mag * jnp.cos(li * dt), mag * jnp.sin(li * dt)
    den = lr * lr + li * li
    nr, ni = ab_re - 1.0, ab_im
    g_re = (nr * lr + ni * li) / den
    g_im = (ni * lr - nr * li) / den
    br, bi = b_re.astype(F32), b_im.astype(F32)
    bb_re = g_re[..., None] * br - g_im[..., None] * bi
    bb_im = g_re[..., None] * bi + g_im[..., None] * br
    eye = jnp.eye(SSM_TILE_GROUPS, dtype=F32)

    def b_tile(bb):
        t = bb.reshape(nt, SSM_TILE_GROUPS, p, ch)
        t = jnp.einsum("ngpc,gh->ngchp", t, eye)
        return t.reshape(nt, SSM_TILE_GROUPS * ch, SSM_TILE_GROUPS * p)

    def c_tile(cc):
        t = cc.reshape(nt, SSM_TILE_GROUPS, ch, p)
        t = jnp.einsum("ngcp,gh->ngphc", t, eye)
        return t.reshape(nt, SSM_TILE_GROUPS * p, SSM_TILE_GROUPS * ch)

    bt = jnp.concatenate([b_tile(bb_re), b_tile(bb_im)], axis=2).astype(BF16)
    ct = jnp.concatenate([c_tile(c_re.astype(F32)), -c_tile(c_im.astype(F32))], axis=1).astype(BF16)
    a_tiles = jnp.stack([ab_re.reshape(nt, SSM_TILE_STATE), ab_im.reshape(nt, SSM_TILE_STATE)], axis=1)
    return bt, ct, a_tiles


def kernel(x, l0_norm_g, l0_w_in, l0_q_norm_g, l0_k_norm_g, l0_lam_q1, l0_lam_k1, l0_lam_q2, l0_lam_k2, l0_head_norm_g, l0_w_out, l1_norm_g, l1_w_in, l1_lam_re, l1_lam_im, l1_log_dt, l1_b_re, l1_b_im, l1_c_re, l1_c_im, l1_d, l1_w_glu, l1_b_glu, l1_w_out):
    b, s, d = x.shape
    e = l0_w_out.shape[0]
    n_heads = e // HEAD_V
    assert s % LANES == 0 and e % MXU_DIM == 0 and b == SUBLANES

    reps = e // SUB_HEAD
    col_scale = jnp.concatenate([
        jnp.tile(l0_q_norm_g.astype(F32) * (SUB_HEAD ** -0.5), reps),
        jnp.tile(l0_k_norm_g.astype(F32), reps),
        jnp.ones((2 * e,), F32)]).reshape(1, 4 * e)
    qkvz = _l0_inproj(x.reshape(b * s, d), l0_norm_g.astype(F32), l0_w_in.astype(BF16),
                      col_scale, 2 * e).reshape(b, s, 4 * e)

    lam_vecs = jnp.stack([l0_lam_q1, l0_lam_k1, l0_lam_q2, l0_lam_k2]).astype(F32)
    o = _l0_attn(qkvz, lam_vecs, l0_head_norm_g.astype(F32), n_heads)

    x1, u_tb, z_tb = _l0_out_l1_in(o, x, l0_w_out.astype(BF16), l1_norm_g.astype(F32),
                                   l1_w_in.astype(BF16))

    bt, ct, a_tiles = _s5_tiles(l1_lam_re, l1_lam_im, l1_log_dt, l1_b_re, l1_b_im, l1_c_re, l1_c_im)
    y_tb = _l1_ssm(u_tb.reshape(s * b, e), bt, ct, a_tiles, b).reshape(s, b * e)

    return _l1_glu_out(y_tb, u_tb, z_tb, x1, l1_d.astype(F32), l1_w_glu.astype(BF16),
                       l1_b_glu.astype(F32), l1_w_out.astype(BF16))
```

```python
import functools
import math

import jax
import jax.numpy as jnp
import numpy as np
from jax import lax
from jax.experimental import pallas as pl
from jax.experimental.pallas import tpu as pltpu

F32 = jnp.float32
BF16 = jnp.bfloat16

EPS = 1e-6
SUB_HEAD = 64
HEAD_V = 2 * SUB_HEAD
GROUP_CH = 16
GROUP_STATE = 64
LAMBDA_INIT_L0 = 0.8 - 0.6 * math.exp(-0.3 * 0)
LOG2E = math.log2(math.e)

LANES = 128
SUBLANES = 8
MXU_DIM = 256
VMEM_LIMIT_BYTES = 56 * 1024 * 1024
NEG_BIG = -1e30
F32_MIN_EXP = 126

ATTN_TQ = 512
ATTN_TQ_ONLINE = 256
FIXED_SHIFT_MAX_LOG2 = (F32_MIN_EXP - 26) / 2.0

SSM_TILE_CH = MXU_DIM
SSM_TILE_GROUPS = SSM_TILE_CH // GROUP_CH
SSM_TILE_STATE = SSM_TILE_GROUPS * GROUP_STATE
SCAN_LANES = 512


def _params(*sem):
    return pltpu.CompilerParams(dimension_semantics=sem, vmem_limit_bytes=VMEM_LIMIT_BYTES)


def _const_spec(shape):
    nd = len(shape)
    return pl.BlockSpec(shape, lambda *_: (0,) * nd, pipeline_mode=pl.Buffered(1))


def _pick(n, pref):
    t = min(n, pref)
    while n % t:
        t //= 2
    return t


def _l0_inproj_kernel(x_ref, g_ref, w_ref, cs_ref, e_ref, o_ref, h_scr, *, n_norm_tiles):
    j = pl.program_id(1)

    @pl.when(j == 0)
    def _():
        xf = x_ref[...]
        ms = jnp.mean(xf * xf, axis=-1, keepdims=True)
        h_scr[...] = (xf * lax.rsqrt(ms + EPS) * g_ref[...]).astype(BF16)

    acc = jnp.dot(h_scr[...], w_ref[...], preferred_element_type=F32)
    tn = acc.shape[1]

    @pl.when(j < n_norm_tiles)
    def _():
        sq = acc * acc
        hi = sq.astype(BF16)
        lo = (sq - hi.astype(F32)).astype(BF16)
        e = e_ref[...]
        for c in range(tn // MXU_DIM):
            sl = slice(c * MXU_DIM, (c + 1) * MXU_DIM)
            ss = (jnp.dot(hi[:, sl], e, preferred_element_type=F32)
                  + jnp.dot(lo[:, sl], e, preferred_element_type=F32))
            inv = lax.rsqrt(ss * (1.0 / SUB_HEAD) + EPS)
            o_ref[:, sl] = (acc[:, sl] * inv * cs_ref[:, sl]).astype(BF16)

    @pl.when(j >= n_norm_tiles)
    def _():
        o_ref[...] = acc.astype(BF16)


def _l0_inproj(x2d, norm_g, w_bf, col_scale, n_norm_cols):
    m, d = x2d.shape
    n = w_bf.shape[1]
    tm = _pick(m, 1024)
    tn = _pick(n_norm_cols, 512)
    seg = np.arange(MXU_DIM) // SUB_HEAD
    e = jnp.asarray(seg[:, None] == seg[None, :], dtype=BF16)
    return pl.pallas_call(
        functools.partial(_l0_inproj_kernel, n_norm_tiles=n_norm_cols // tn),
        out_shape=jax.ShapeDtypeStruct((m, n), BF16),
        grid=(m // tm, n // tn),
        in_specs=[
            pl.BlockSpec((tm, d), lambda i, j: (i, 0)),
            _const_spec((1, d)),
            pl.BlockSpec((d, tn), lambda i, j: (0, j)),
            pl.BlockSpec((1, tn), lambda i, j: (0, j)),
            _const_spec((MXU_DIM, MXU_DIM)),
        ],
        out_specs=pl.BlockSpec((tm, tn), lambda i, j: (i, j)),
        scratch_shapes=[pltpu.VMEM((tm, d), BF16)],
        compiler_params=_params("parallel", "arbitrary"),
        name="l0_inproj",
    )(x2d, norm_g.reshape(1, d), w_bf, col_scale, e)


def _split3(val):
    p1 = val.astype(BF16).astype(F32)
    r1 = val - p1
    p2 = r1.astype(BF16).astype(F32)
    p3 = (r1 - p2).astype(BF16).astype(F32)
    return p1, p2, p3


def _augment(parts, lane, off, ones_first):
    p_off = off + 3 if ones_first else off
    o_off = off if ones_first else off + 3
    a = jnp.where(lane == p_off, parts[0], 0.0)
    a = jnp.where(lane == p_off + 1, parts[1], a)
    a = jnp.where(lane == p_off + 2, parts[2], a)
    for t in range(3):
        a = jnp.where(lane == o_off + t, 1.0, a)
    return a


def _sub_head_sq_norms(xf, low_half):
    x2 = xf * xf
    return (jnp.sum(jnp.where(low_half, x2, 0.0), axis=1, keepdims=True),
            jnp.sum(jnp.where(low_half, 0.0, x2), axis=1, keepdims=True))


def _attn_kernel(q_ref, k_ref, v_ref, z_ref, lam_ref, hg_ref, slope_ref, o_ref,
                 kaug_scr, vaug_scr, qaug_scr, acc_scr, acc1_scr, m_scr, l_scr, *, tq, tq1):
    s_len = q_ref.shape[0]
    half = tq // 2
    h = pl.program_id(1)
    slope = slope_ref[pl.ds(h, 1), :] * LOG2E
    lane = lax.broadcasted_iota(jnp.int32, (tq, LANES), 1)
    row = lax.broadcasted_iota(jnp.int32, (tq, LANES), 0)
    low_half = lane < SUB_HEAD

    lam_v = lam_ref[...]
    lam = (jnp.exp(jnp.sum(lam_v[0:1] * lam_v[1:2], axis=1, keepdims=True))
           - jnp.exp(jnp.sum(lam_v[2:3] * lam_v[3:4], axis=1, keepdims=True)) + LAMBDA_INIT_L0)

    def finish(r0, n_rows, o0, l0, o1, l1):
        o = o0 * (1.0 / l0) - lam * (o1 * (1.0 / l1))
        ms = jnp.mean(o * o, axis=-1, keepdims=True)
        on = o * lax.rsqrt(ms + EPS) * hg_ref[...] * (1.0 - LAMBDA_INIT_L0)
        z = z_ref[pl.ds(r0, n_rows), :].astype(F32)
        o_ref[pl.ds(r0, n_rows), :] = (on * (z / (1.0 + jnp.exp(-z)))).astype(BF16)

    ones_col = jnp.where(lane == 0, 1.0, 0.0).astype(BF16)

    def build(c, carry):
        r0 = pl.multiple_of(c * tq, tq)
        kf = k_ref[pl.ds(r0, tq), :].astype(F32)
        parts = _split3(slope * (r0 + row).astype(F32))
        kaug_scr[0, pl.ds(r0, tq), :] = jnp.where(
            low_half, kf, _augment(parts, lane, SUB_HEAD, True)).astype(BF16)
        kaug_scr[1, pl.ds(r0, tq), :] = jnp.where(
            low_half, _augment(parts, lane, 0, True), kf).astype(BF16)
        vaug_scr[pl.ds(r0, tq), 0:HEAD_V] = v_ref[pl.ds(r0, tq), :]
        vaug_scr[pl.ds(r0, tq), HEAD_V:2 * HEAD_V] = ones_col
        k_sq = _sub_head_sq_norms(kf, low_half)
        q_sq = _sub_head_sq_norms(q_ref[pl.ds(r0, tq), :].astype(F32), low_half)
        return tuple(jnp.maximum(a, b) for a, b in zip(carry, k_sq + q_sq))

    zero = jnp.zeros((tq, 1), F32)
    sq_max = lax.fori_loop(0, s_len // tq, build, (zero, zero, zero, zero))
    k_max0, k_max1, q_max0, q_max1 = (jnp.sqrt(jnp.max(a, axis=0, keepdims=True)) for a in sq_max)
    worst = jnp.maximum(q_max0 * k_max0, q_max1 * k_max1)
    fixed_shift_ok = worst[0, 0] <= FIXED_SHIFT_MAX_LOG2

    @pl.when(fixed_shift_ok)
    def _fixed_shift():
        def q_block(i, carry):
            r0 = pl.multiple_of(i * tq, tq)
            qf = q_ref[pl.ds(r0, tq), :].astype(F32)
            q_sq0, q_sq1 = _sub_head_sq_norms(qf, low_half)
            pos = slope * (r0 + row).astype(F32)
            off0 = _split3(-(pos + jnp.sqrt(q_sq0) * k_max0))
            off1 = _split3(-(pos + jnp.sqrt(q_sq1) * k_max1))
            qaug_scr[0] = jnp.where(low_half, qf, _augment(off0, lane, SUB_HEAD, False)).astype(BF16)
            qaug_scr[1] = jnp.where(low_half, _augment(off1, lane, 0, False), qf).astype(BF16)
            acc_scr[...] = jnp.zeros(acc_scr.shape, F32)

            def step(row_lo, n_rows, c0, width, mask_off):
                for sub in range(2):
                    qa = qaug_scr[sub, row_lo:row_lo + n_rows, :]
                    kk = kaug_scr[sub, pl.ds(c0, width), :]
                    s = lax.dot_general(qa, kk, (((1,), (1,)), ((), ())),
                                        preferred_element_type=F32)
                    if mask_off is not None:
                        cols = lax.broadcasted_iota(jnp.int32, (n_rows, width), 1)
                        rows = lax.broadcasted_iota(jnp.int32, (n_rows, width), 0)
                        s = jnp.where(cols <= rows + mask_off, s, -jnp.inf)
                    p = jnp.exp2(s).astype(BF16)
                    acc_scr[sub, row_lo:row_lo + n_rows, :] += jnp.dot(
                        p, vaug_scr[pl.ds(c0, width), :], preferred_element_type=F32)

            def off_diag(j, c):
                step(0, tq, pl.multiple_of(j * tq, tq), tq, None)
                return c

            lax.fori_loop(0, i, off_diag, 0)
            step(0, half, r0, half, 0)
            step(half, half, r0, tq, half)
            a0 = acc_scr[0]
            a1 = acc_scr[1]
            finish(r0, tq, a0[:, :HEAD_V], a0[:, HEAD_V:HEAD_V + 1], a1[:, :HEAD_V], a1[:, HEAD_V:HEAD_V + 1])
            return carry

        lax.fori_loop(0, s_len // tq, q_block, 0)

    @pl.when(jnp.logical_not(fixed_shift_ok))
    def _online():
        lane1 = lax.broadcasted_iota(jnp.int32, (tq1, LANES), 1)
        row1 = lax.broadcasted_iota(jnp.int32, (tq1, LANES), 0)
        low1 = lane1 < SUB_HEAD
        col_s = lax.broadcasted_iota(jnp.int32, (tq1, tq1), 1)
        row_s = lax.broadcasted_iota(jnp.int32, (tq1, tq1), 0)
        causal = col_s <= row_s

        def q_block(i, carry):
            r0 = pl.multiple_of(i * tq1, tq1)
            qf = q_ref[pl.ds(r0, tq1), :].astype(F32)
            parts = _split3(-slope * (r0 + row1).astype(F32))
            q_aug = (jnp.where(low1, qf, _augment(parts, lane1, SUB_HEAD, False)).astype(BF16),
                     jnp.where(low1, _augment(parts, lane1, 0, False), qf).astype(BF16))
            m_scr[...] = jnp.full(m_scr.shape, NEG_BIG, F32)
            l_scr[...] = jnp.zeros(l_scr.shape, F32)
            acc1_scr[...] = jnp.zeros(acc1_scr.shape, F32)

            def kv_step(j, masked):
                c0 = pl.multiple_of(j * tq1, tq1)
                v = v_ref[pl.ds(c0, tq1), :]
                for sub in range(2):
                    kk = kaug_scr[sub, pl.ds(c0, tq1), :]
                    s = lax.dot_general(q_aug[sub], kk, (((1,), (1,)), ((), ())),
                                        preferred_element_type=F32)
                    if masked:
                        s = jnp.where(causal, s, -jnp.inf)
                    m_prev = m_scr[sub]
                    m_new = jnp.maximum(m_prev, jnp.max(s, axis=1, keepdims=True))
                    alpha = jnp.exp2(m_prev - m_new)
                    p = jnp.exp2(s - pltpu.repeat(m_new, tq1 // LANES, axis=1))
                    l_scr[sub] = alpha * l_scr[sub] + jnp.sum(p, axis=1, keepdims=True)
                    acc1_scr[sub] = alpha * acc1_scr[sub] + jnp.dot(
                        p.astype(BF16), v, preferred_element_type=F32)
                    m_scr[sub] = m_new

            def off_diag(j, c):
                kv_step(j, False)
                return c

            lax.fori_loop(0, i, off_diag, 0)
            kv_step(i, True)
            finish(r0, tq1, acc1_scr[0], l_scr[0], acc1_scr[1], l_scr[1])
            return carry

        lax.fori_loop(0, s_len // tq1, q_block, 0)


def _l0_attn(qkvz, lam_vecs, head_g, n_heads):
    b, s, _ = qkvz.shape
    tq = _pick(s, ATTN_TQ)
    tq1 = _pick(s, ATTN_TQ_ONLINE)
    slopes = 2.0 ** (-8.0 * np.arange(1, n_heads + 1) / n_heads)
    slope_tab = jnp.asarray(np.broadcast_to(slopes[:, None], (n_heads, LANES)), dtype=F32)

    def head_block(which):
        return pl.BlockSpec((None, s, HEAD_V), lambda bi, hi: (bi, 0, which * n_heads + hi))

    return pl.pallas_call(
        functools.partial(_attn_kernel, tq=tq, tq1=tq1),
        out_shape=jax.ShapeDtypeStruct((b, s, n_heads * HEAD_V), BF16),
        grid=(b, n_heads),
        in_specs=[head_block(0), head_block(1), head_block(2), head_block(3),
                  _const_spec((4, SUB_HEAD)), _const_spec((1, HEAD_V)),
                  _const_spec((n_heads, LANES))],
        out_specs=pl.BlockSpec((None, s, HEAD_V), lambda bi, hi: (bi, 0, hi)),
        scratch_shapes=[pltpu.VMEM((2, s, HEAD_V), BF16),
                        pltpu.VMEM((s, 2 * HEAD_V), BF16),
                        pltpu.VMEM((2, tq, HEAD_V), BF16),
                        pltpu.VMEM((2, tq, 2 * HEAD_V), F32),
                        pltpu.VMEM((2, tq1, HEAD_V), F32),
                        pltpu.VMEM((2, tq1, LANES), F32),
                        pltpu.VMEM((2, tq1, LANES), F32)],
        compiler_params=_params("parallel", "parallel"),
        name="l0_attn",
    )(qkvz, qkvz, qkvz, qkvz, lam_vecs, head_g.reshape(1, HEAD_V), slope_tab)


def _l0_out_l1_in_kernel(o_ref, x_ref, wo_ref, g_ref, wi_ref, x1_ref, u_ref, z_ref):
    e = u_ref.shape[1]
    x1 = x_ref[...] + jnp.dot(o_ref[...], wo_ref[...], preferred_element_type=F32)
    x1_ref[...] = x1
    ms = jnp.mean(x1 * x1, axis=-1, keepdims=True)
    hn = (x1 * lax.rsqrt(ms + EPS) * g_ref[...]).astype(BF16)
    u_ref[...] = jnp.dot(hn, wi_ref[:, :e], preferred_element_type=F32).astype(BF16)
    z_ref[...] = jnp.dot(hn, wi_ref[:, e:], preferred_element_type=F32).astype(BF16)


def _l0_out_l1_in(o, x, wo_bf, norm_g, wi_bf):
    b, s, d = x.shape
    e = o.shape[2]
    tm = _pick(s, 512)
    tb_shape = jax.ShapeDtypeStruct((s, b * e), BF16)
    return pl.pallas_call(
        _l0_out_l1_in_kernel,
        out_shape=(jax.ShapeDtypeStruct((b, s, d), F32), tb_shape, tb_shape),
        grid=(b, s // tm),
        in_specs=[
            pl.BlockSpec((None, tm, e), lambda bi, si: (bi, si, 0)),
            pl.BlockSpec((None, tm, d), lambda bi, si: (bi, si, 0)),
            _const_spec((e, d)),
            _const_spec((1, d)),
            _const_spec((d, 2 * e)),
        ],
        out_specs=(
            pl.BlockSpec((None, tm, d), lambda bi, si: (bi, si, 0)),
            pl.BlockSpec((tm, e), lambda bi, si: (si, bi)),
            pl.BlockSpec((tm, e), lambda bi, si: (si, bi)),
        ),
        compiler_params=_params("parallel", "parallel"),
        name="l0_out_l1_in",
    )(o, x, wo_bf, norm_g.reshape(1, d), wi_bf)


def _ssm_kernel(u_ref, bt_ref, ct_ref, a_ref, y_ref, st_scr, h_scr, *, batch):
    c = pl.program_id(1)
    n_steps = u_ref.shape[0] // batch

    @pl.when(c == 0)
    def _():
        h_scr[...] = jnp.zeros(h_scr.shape, F32)

    st_scr[...] = jnp.dot(u_ref[...], bt_ref[...], preferred_element_type=F32)

    for w in range(SSM_TILE_STATE // SCAN_LANES):
        re = pl.ds(w * SCAN_LANES, SCAN_LANES)
        im = pl.ds(SSM_TILE_STATE + w * SCAN_LANES, SCAN_LANES)
        a_re = jnp.broadcast_to(a_ref[0:1, re], (batch, SCAN_LANES))
        a_im = jnp.broadcast_to(a_ref[1:2, re], (batch, SCAN_LANES))

        def step(t, carry):
            h_re, h_im = carry
            rows = pl.ds(pl.multiple_of(t * batch, batch), batch)
            n_re = a_re * h_re - a_im * h_im + st_scr[rows, re]
            n_im = a_re * h_im + a_im * h_re + st_scr[rows, im]
            st_scr[rows, re] = n_re
            st_scr[rows, im] = n_im
            return n_re, n_im

        h_re, h_im = lax.fori_loop(0, n_steps, step, (h_scr[:, re], h_scr[:, im]), unroll=4)
        h_scr[:, re] = h_re
        h_scr[:, im] = h_im

    y_ref[...] = jnp.dot(st_scr[...].astype(BF16), ct_ref[...],
                         preferred_element_type=F32).astype(BF16)


def _l1_ssm(u_tb, bt, ct, a_tiles, batch):
    rows, e = u_tb.shape
    n_tiles = e // SSM_TILE_CH
    steps = _pick(rows // batch, 128)
    blk = steps * batch
    return pl.pallas_call(
        functools.partial(_ssm_kernel, batch=batch),
        out_shape=jax.ShapeDtypeStruct((rows, e), BF16),
        grid=(n_tiles, rows // blk),
        in_specs=[
            pl.BlockSpec((blk, SSM_TILE_CH), lambda i, c: (c, i)),
            pl.BlockSpec((None, SSM_TILE_CH, 2 * SSM_TILE_STATE), lambda i, c: (i, 0, 0)),
            pl.BlockSpec((None, 2 * SSM_TILE_STATE, SSM_TILE_CH), lambda i, c: (i, 0, 0)),
            pl.BlockSpec((None, 2, SSM_TILE_STATE), lambda i, c: (i, 0, 0)),
        ],
        out_specs=pl.BlockSpec((blk, SSM_TILE_CH), lambda i, c: (c, i)),
        scratch_shapes=[pltpu.VMEM((blk, 2 * SSM_TILE_STATE), F32),
                        pltpu.VMEM((batch, 2 * SSM_TILE_STATE), F32)],
        compiler_params=_params("parallel", "arbitrary"),
        name="l1_ssm",
    )(u_tb, bt, ct, a_tiles)


def _glu_out_kernel(y_ref, u_ref, z_ref, x1_ref, d_ref, wg_ref, bg_ref, wo_ref, o_ref):
    y = y_ref[...].astype(F32) + d_ref[...] * u_ref[...].astype(F32)
    cdf = 0.5 * (1.0 + jnp.tanh(math.sqrt(2.0 / math.pi) * (y + 0.044715 * (y * y * y))))
    y = y * cdf
    gate = jnp.dot(y.astype(BF16), wg_ref[...], preferred_element_type=F32) + bg_ref[...]
    y = y * (1.0 / (1.0 + jnp.exp(-gate)))
    z = z_ref[...].astype(F32)
    y = y * (z / (1.0 + jnp.exp(-z)))
    o_ref[...] = x1_ref[...] + jnp.dot(y.astype(BF16), wo_ref[...], preferred_element_type=F32)


def _l1_glu_out(y_tb, u_tb, z_tb, x1, d_skip, wg_bf, b_glu, wo_bf):
    b, s, d = x1.shape
    e = wg_bf.shape[0]
    ts = _pick(s, 256)
    tb_spec = pl.BlockSpec((ts, e), lambda bi, si: (si, bi))
    return pl.pallas_call(
        _glu_out_kernel,
        out_shape=jax.ShapeDtypeStruct((b, s, d), F32),
        grid=(b, s // ts),
        in_specs=[tb_spec, tb_spec, tb_spec,
                  pl.BlockSpec((None, ts, d), lambda bi, si: (bi, si, 0)),
                  _const_spec((1, e)), _const_spec((e, e)), _const_spec((1, e)),
                  _const_spec((e, d))],
        out_specs=pl.BlockSpec((None, ts, d), lambda bi, si: (bi, si, 0)),
        compiler_params=_params("parallel", "parallel"),
        name="l1_glu_out",
    )(y_tb, u_tb, z_tb, x1, d_skip.reshape(1, e), wg_bf, b_glu.reshape(1, e), wo_bf)


def _s5_tiles(lam_re, lam_im, log_dt, b_re, b_im, c_re, c_im):
    g, p = lam_re.shape
    ch = b_re.shape[2]
    nt = g // SSM_TILE_GROUPS
    dt = jnp.exp(log_dt.astype(F32))[:, None]
    lr, li = lam_re.astype(F32), lam_im.astype(F32)
    mag = jnp.exp(lr * dt)
    ab_re, ab_im = mag * jnp.cos(li * dt), mag * jnp.sin(li * dt)
    den = lr * lr + li * li
    nr, ni = ab_re - 1.0, ab_im
    g_re = (nr * lr + ni * li) / den
    g_im = (ni * lr - nr * li) / den
    br, bi = b_re.astype(F32), b_im.astype(F32)
    bb_re = g_re[..., None] * br - g_im[..., None] * bi
    bb_im = g_re[..., None] * bi + g_im[..., None] * br
    eye = jnp.eye(SSM_TILE_GROUPS, dtype=F32)

    def b_tile(bb):
        t = bb.reshape(nt, SSM_TILE_GROUPS, p, ch)
        t = jnp.einsum("ngpc,gh->ngchp", t, eye)
        return t.reshape(nt, SSM_TILE_GROUPS * ch, SSM_TILE_GROUPS * p)

    def c_tile(cc):
        t = cc.reshape(nt, SSM_TILE_GROUPS, ch, p)
        t = jnp.einsum("ngcp,gh->ngphc", t, eye)
        return t.reshape(nt, SSM_TILE_GROUPS * p, SSM_TILE_GROUPS * ch)

    bt = jnp.concatenate([b_tile(bb_re), b_tile(bb_im)], axis=2).astype(BF16)
    ct = jnp.concatenate([c_tile(c_re.astype(F32)), -c_tile(c_im.astype(F32))], axis=1).astype(BF16)
    a_tiles = jnp.stack([ab_re.reshape(nt, SSM_TILE_STATE), ab_im.reshape(nt, SSM_TILE_STATE)], axis=1)
    return bt, ct, a_tiles


def kernel(x, l0_norm_g, l0_w_in, l0_q_norm_g, l0_k_norm_g, l0_lam_q1, l0_lam_k1, l0_lam_q2, l0_lam_k2, l0_head_norm_g, l0_w_out, l1_norm_g, l1_w_in, l1_lam_re, l1_lam_im, l1_log_dt, l1_b_re, l1_b_im, l1_c_re, l1_c_im, l1_d, l1_w_glu, l1_b_glu, l1_w_out):
    b, s, d = x.shape
    e = l0_w_out.shape[0]
    n_heads = e // HEAD_V
    assert s % LANES == 0 and e % MXU_DIM == 0 and b == SUBLANES

    reps = e // SUB_HEAD
    col_scale = jnp.concatenate([
        jnp.tile(l0_q_norm_g.astype(F32) * (SUB_HEAD ** -0.5 * LOG2E), reps),
        jnp.tile(l0_k_norm_g.astype(F32), reps),
        jnp.ones((2 * e,), F32)]).reshape(1, 4 * e)
    qkvz = _l0_inproj(x.reshape(b * s, d), l0_norm_g.astype(F32), l0_w_in.astype(BF16),
                      col_scale, 2 * e).reshape(b, s, 4 * e)

    lam_vecs = jnp.stack([l0_lam_q1, l0_lam_k1, l0_lam_q2, l0_lam_k2]).astype(F32)
    o = _l0_attn(qkvz, lam_vecs, l0_head_norm_g.astype(F32), n_heads)

    x1, u_tb, z_tb = _l0_out_l1_in(o, x, l0_w_out.astype(BF16), l1_norm_g.astype(F32),
                                   l1_w_in.astype(BF16))

    bt, ct, a_tiles = _s5_tiles(l1_lam_re, l1_lam_im, l1_log_dt, l1_b_re, l1_b_im, l1_c_re, l1_c_im)
    y_tb = _l1_ssm(u_tb.reshape(s * b, e), bt, ct, a_tiles, b).reshape(s, b * e)

    return _l1_glu_out(y_tb, u_tb, z_tb, x1, l1_d.astype(F32), l1_w_glu.astype(BF16),
                       l1_b_glu.astype(F32), l1_w_out.astype(BF16))
```

```python
import functools
import math

import jax
import jax.numpy as jnp
import numpy as np
from jax import lax
from jax.experimental import pallas as pl
from jax.experimental.pallas import tpu as pltpu

F32 = jnp.float32
BF16 = jnp.bfloat16

EPS = 1e-6
SUB_HEAD = 64
HEAD_V = 2 * SUB_HEAD
GROUP_CH = 16
GROUP_STATE = 64
LAMBDA_INIT_L0 = 0.8 - 0.6 * math.exp(-0.3 * 0)
LOG2E = math.log2(math.e)

LANES = 128
SUBLANES = 8
MXU_DIM = 256
VMEM_LIMIT_BYTES = 56 * 1024 * 1024
NEG_BIG = -1e30
F32_MIN_EXP = 126

ATTN_TQ = 512
ATTN_TQ_ONLINE = 256
FIXED_SHIFT_MAX_LOG2 = (F32_MIN_EXP - 26) / 2.0
BF16_ROUNDING_SLACK = 1.01

SSM_TILE_CH = MXU_DIM
SSM_TILE_GROUPS = SSM_TILE_CH // GROUP_CH
SSM_TILE_STATE = SSM_TILE_GROUPS * GROUP_STATE
SCAN_LANES = 512


def _params(*sem):
    return pltpu.CompilerParams(dimension_semantics=sem, vmem_limit_bytes=VMEM_LIMIT_BYTES)


def _const_spec(shape):
    nd = len(shape)
    return pl.BlockSpec(shape, lambda *_: (0,) * nd, pipeline_mode=pl.Buffered(1))


def _pick(n, pref):
    t = min(n, pref)
    while n % t:
        t //= 2
    return t


def _l0_inproj_kernel(x_ref, g_ref, w_ref, cs_ref, e_ref, o_ref, h_scr, *, n_norm_tiles):
    j = pl.program_id(1)

    @pl.when(j == 0)
    def _():
        xf = x_ref[...]
        ms = jnp.mean(xf * xf, axis=-1, keepdims=True)
        h_scr[...] = (xf * lax.rsqrt(ms + EPS) * g_ref[...]).astype(BF16)

    acc = jnp.dot(h_scr[...], w_ref[...], preferred_element_type=F32)
    tn = acc.shape[1]

    @pl.when(j < n_norm_tiles)
    def _():
        sq = acc * acc
        hi = sq.astype(BF16)
        lo = (sq - hi.astype(F32)).astype(BF16)
        e = e_ref[...]
        for c in range(tn // MXU_DIM):
            sl = slice(c * MXU_DIM, (c + 1) * MXU_DIM)
            ss = (jnp.dot(hi[:, sl], e, preferred_element_type=F32)
                  + jnp.dot(lo[:, sl], e, preferred_element_type=F32))
            inv = lax.rsqrt(ss * (1.0 / SUB_HEAD) + EPS)
            o_ref[:, sl] = (acc[:, sl] * inv * cs_ref[:, sl]).astype(BF16)

    @pl.when(j >= n_norm_tiles)
    def _():
        o_ref[...] = acc.astype(BF16)


def _l0_inproj(x2d, norm_g, w_bf, col_scale, n_norm_cols):
    m, d = x2d.shape
    n = w_bf.shape[1]
    tm = _pick(m, 1024)
    tn = _pick(n_norm_cols, 512)
    seg = np.arange(MXU_DIM) // SUB_HEAD
    e = jnp.asarray(seg[:, None] == seg[None, :], dtype=BF16)
    return pl.pallas_call(
        functools.partial(_l0_inproj_kernel, n_norm_tiles=n_norm_cols // tn),
        out_shape=jax.ShapeDtypeStruct((m, n), BF16),
        grid=(m // tm, n // tn),
        in_specs=[
            pl.BlockSpec((tm, d), lambda i, j: (i, 0)),
            _const_spec((1, d)),
            pl.BlockSpec((d, tn), lambda i, j: (0, j)),
            pl.BlockSpec((1, tn), lambda i, j: (0, j)),
            _const_spec((MXU_DIM, MXU_DIM)),
        ],
        out_specs=pl.BlockSpec((tm, tn), lambda i, j: (i, j)),
        scratch_shapes=[pltpu.VMEM((tm, d), BF16)],
        compiler_params=_params("parallel", "arbitrary"),
        name="l0_inproj",
    )(x2d, norm_g.reshape(1, d), w_bf, col_scale, e)


def _split3(val):
    p1 = val.astype(BF16).astype(F32)
    r1 = val - p1
    p2 = r1.astype(BF16).astype(F32)
    p3 = (r1 - p2).astype(BF16).astype(F32)
    return p1, p2, p3


def _augment(parts, lane, off, ones_first):
    p_off = off + 3 if ones_first else off
    o_off = off if ones_first else off + 3
    a = jnp.where(lane == p_off, parts[0], 0.0)
    a = jnp.where(lane == p_off + 1, parts[1], a)
    a = jnp.where(lane == p_off + 2, parts[2], a)
    for t in range(3):
        a = jnp.where(lane == o_off + t, 1.0, a)
    return a


def _scores(qa, kk):
    return lax.dot_general(qa, kk, (((1,), (1,)), ((), ())), preferred_element_type=F32)


def _attn_kernel(pairs_ref, shift_ref, q_ref, k_ref, v_ref, z_ref, lam_ref, hg_ref, slope_ref, o_ref,
                 aq_scr, ak_scr, qaug_scr, kaug_scr, vaug_scr, acc_scr, acc1_scr, m_scr, l_scr,
                 *, tq, tq1, n_pairs):
    s_len = q_ref.shape[0]
    n_blk = s_len // tq
    half = tq // 2
    h = pl.program_id(0)
    shift = shift_ref[0]
    slope = slope_ref[pl.ds(h, 1), :] * LOG2E
    lane = lax.broadcasted_iota(jnp.int32, (tq, LANES), 1)
    row = lax.broadcasted_iota(jnp.int32, (tq, LANES), 0)

    lam_v = lam_ref[...]
    lam = (jnp.exp(jnp.sum(lam_v[0:1] * lam_v[1:2], axis=1, keepdims=True))
           - jnp.exp(jnp.sum(lam_v[2:3] * lam_v[3:4], axis=1, keepdims=True)) + LAMBDA_INIT_L0)

    def blk(i, size):
        return pl.ds(pl.multiple_of(i * size, size), size)

    def finish(rows, o0, l0, o1, l1):
        o = o0 * (1.0 / l0) - lam * (o1 * (1.0 / l1))
        ms = jnp.mean(o * o, axis=-1, keepdims=True)
        on = o * lax.rsqrt(ms + EPS) * hg_ref[...] * (1.0 - LAMBDA_INIT_L0)
        z = z_ref[rows, :].astype(F32)
        o_ref[rows, :] = (on * (z / (1.0 + jnp.exp(-z)))).astype(BF16)

    @pl.when(pl.program_id(1) == 0)
    def _tables():
        ones_col = jnp.ones((tq, HEAD_V), BF16)

        def body(c, carry):
            rows = blk(c, tq)
            pos = slope * (c * tq + row).astype(F32)
            k_parts = _split3(pos)
            q_parts = _split3(-(pos + shift))
            ak_scr[0, rows, :] = _augment(k_parts, lane, SUB_HEAD, True).astype(BF16)
            ak_scr[1, rows, :] = _augment(k_parts, lane, 0, True).astype(BF16)
            aq_scr[0, rows, :] = _augment(q_parts, lane, SUB_HEAD, False).astype(BF16)
            aq_scr[1, rows, :] = _augment(q_parts, lane, 0, False).astype(BF16)
            vaug_scr[rows, HEAD_V:2 * HEAD_V] = ones_col
            return carry

        lax.fori_loop(0, n_blk, body, 0)

    lane_row = lax.broadcasted_iota(jnp.int32, (1, LANES), 1)
    keep = (jnp.where(lane_row < SUB_HEAD, 1.0, 0.0).astype(BF16),
            jnp.where(lane_row < SUB_HEAD, 0.0, 1.0).astype(BF16))

    def build(c, carry):
        rows = blk(c, tq)
        kb = k_ref[rows, :]
        qb = q_ref[rows, :]
        for sub in range(2):
            kaug_scr[sub, rows, :] = kb * keep[sub] + ak_scr[sub, rows, :]
            qaug_scr[sub, rows, :] = qb * keep[sub] + aq_scr[sub, rows, :]
        vaug_scr[rows, 0:HEAD_V] = v_ref[rows, :]
        return carry

    lax.fori_loop(0, n_blk, build, 0)

    fixed_shift_ok = shift <= FIXED_SHIFT_MAX_LOG2

    @pl.when(fixed_shift_ok)
    def _fixed_shift():
        def tile(q_rows, k_rows, mask_off, init):
            n_q, n_k = q_rows.size, k_rows.size
            for sub in range(2):
                s = _scores(qaug_scr[sub, q_rows, :], kaug_scr[sub, k_rows, :])
                if mask_off is not None:
                    cols = lax.broadcasted_iota(jnp.int32, (n_q, n_k), 1)
                    rws = lax.broadcasted_iota(jnp.int32, (n_q, n_k), 0)
                    s = jnp.where(cols <= rws + mask_off, s, -jnp.inf)
                pv = jnp.dot(jnp.exp2(s).astype(BF16), vaug_scr[k_rows, :], preferred_element_type=F32)
                if init:
                    acc_scr[sub, q_rows, :] = pv
                else:
                    acc_scr[sub, q_rows, :] += pv

        def diagonal(i, carry):
            r0 = pl.multiple_of(i * tq, tq)
            tile(pl.ds(r0, half), pl.ds(r0, half), 0, True)
            tile(pl.ds(r0 + half, half), pl.ds(r0, tq), half, True)
            return carry

        lax.fori_loop(0, n_blk, diagonal, 0)

        def off_diagonal(p):
            tile(blk(pairs_ref[2 * p], tq), blk(pairs_ref[2 * p + 1], tq), None, False)

        def two_pairs(t, carry):
            off_diagonal(2 * t)
            off_diagonal(2 * t + 1)
            return carry

        lax.fori_loop(0, n_pairs // 2, two_pairs, 0)
        if n_pairs % 2:
            off_diagonal(n_pairs - 1)

        def finalize(i, carry):
            rows = blk(i, tq)
            a0 = acc_scr[0, rows, :]
            a1 = acc_scr[1, rows, :]
            finish(rows, a0[:, :HEAD_V], a0[:, HEAD_V:], a1[:, :HEAD_V], a1[:, HEAD_V:])
            return carry

        lax.fori_loop(0, n_blk, finalize, 0)

    @pl.when(jnp.logical_not(fixed_shift_ok))
    def _online():
        col_s = lax.broadcasted_iota(jnp.int32, (tq1, tq1), 1)
        row_s = lax.broadcasted_iota(jnp.int32, (tq1, tq1), 0)
        causal = col_s <= row_s

        def q_block(i, carry):
            rows = blk(i, tq1)
            m_scr[...] = jnp.full(m_scr.shape, NEG_BIG, F32)
            l_scr[...] = jnp.zeros(l_scr.shape, F32)
            acc1_scr[...] = jnp.zeros(acc1_scr.shape, F32)

            def kv_step(j, masked):
                k_rows = blk(j, tq1)
                v = v_ref[k_rows, :]
                for sub in range(2):
                    s = _scores(qaug_scr[sub, rows, :], kaug_scr[sub, k_rows, :])
                    if masked:
                        s = jnp.where(causal, s, -jnp.inf)
                    m_prev = m_scr[sub]
                    m_new = jnp.maximum(m_prev, jnp.max(s, axis=1, keepdims=True))
                    alpha = jnp.exp2(m_prev - m_new)
                    p = jnp.exp2(s - m_new)
                    l_scr[sub] = alpha * l_scr[sub] + jnp.sum(p, axis=1, keepdims=True)
                    acc1_scr[sub] = alpha * acc1_scr[sub] + jnp.dot(
                        p.astype(BF16), v, preferred_element_type=F32)
                    m_scr[sub] = m_new

            def off_diag(j, c):
                kv_step(j, False)
                return c

            lax.fori_loop(0, i, off_diag, 0)
            kv_step(i, True)
            finish(rows, acc1_scr[0], l_scr[0], acc1_scr[1], l_scr[1])
            return carry

        lax.fori_loop(0, s_len // tq1, q_block, 0)


def _l0_attn(qkvz, lam_vecs, head_g, score_bound, n_heads):
    b, s, _ = qkvz.shape
    tq = _pick(s, ATTN_TQ)
    tq1 = _pick(s, ATTN_TQ_ONLINE)
    n_blk = s // tq
    slopes = 2.0 ** (-8.0 * np.arange(1, n_heads + 1) / n_heads)
    slope_tab = jnp.asarray(np.broadcast_to(slopes[:, None], (n_heads, LANES)), dtype=F32)
    pairs = np.asarray([(i, j) for j in range(n_blk) for i in range(j + 1, n_blk)],
                       dtype=np.int32).reshape(-1)
    n_pairs = pairs.size // 2
    pairs = jnp.asarray(np.concatenate([pairs, np.zeros(2, np.int32)]))

    def head_block(which):
        return pl.BlockSpec((None, s, HEAD_V), lambda hi, bi: (bi, 0, which * n_heads + hi))

    smem = pl.BlockSpec(memory_space=pltpu.SMEM)
    return pl.pallas_call(
        functools.partial(_attn_kernel, tq=tq, tq1=tq1, n_pairs=n_pairs),
        out_shape=jax.ShapeDtypeStruct((b, s, n_heads * HEAD_V), BF16),
        grid=(n_heads, b),
        in_specs=[smem, smem, head_block(0), head_block(1), head_block(2), head_block(3),
                  _const_spec((4, SUB_HEAD)), _const_spec((1, HEAD_V)),
                  _const_spec((n_heads, LANES))],
        out_specs=pl.BlockSpec((None, s, HEAD_V), lambda hi, bi: (bi, 0, hi)),
        scratch_shapes=[pltpu.VMEM((2, s, HEAD_V), BF16),
                        pltpu.VMEM((2, s, HEAD_V), BF16),
                        pltpu.VMEM((2, s, HEAD_V), BF16),
                        pltpu.VMEM((2, s, HEAD_V), BF16),
                        pltpu.VMEM((s, 2 * HEAD_V), BF16),
                        pltpu.VMEM((2, s, 2 * HEAD_V), F32),
                        pltpu.VMEM((2, tq1, HEAD_V), F32),
                        pltpu.VMEM((2, tq1, 1), F32),
                        pltpu.VMEM((2, tq1, 1), F32)],
        compiler_params=_params("parallel", "arbitrary"),
        name="l0_attn",
    )(pairs, score_bound, qkvz, qkvz, qkvz, qkvz, lam_vecs, head_g.reshape(1, HEAD_V), slope_tab)


def _l0_out_l1_in_kernel(o_ref, x_ref, wo_ref, g_ref, wi_ref, x1_ref, u_ref, z_ref):
    e = u_ref.shape[1]
    x1 = x_ref[...] + jnp.dot(o_ref[...], wo_ref[...], preferred_element_type=F32)
    x1_ref[...] = x1
    ms = jnp.mean(x1 * x1, axis=-1, keepdims=True)
    hn = (x1 * lax.rsqrt(ms + EPS) * g_ref[...]).astype(BF16)
    u_ref[...] = jnp.dot(hn, wi_ref[:, :e], preferred_element_type=F32).astype(BF16)
    z_ref[...] = jnp.dot(hn, wi_ref[:, e:], preferred_element_type=F32).astype(BF16)


def _l0_out_l1_in(o, x, wo_bf, norm_g, wi_bf):
    b, s, d = x.shape
    e = o.shape[2]
    tm = _pick(s, 512)
    tb_shape = jax.ShapeDtypeStruct((s, b * e), BF16)
    return pl.pallas_call(
        _l0_out_l1_in_kernel,
        out_shape=(jax.ShapeDtypeStruct((b, s, d), F32), tb_shape, tb_shape),
        grid=(b, s // tm),
        in_specs=[
            pl.BlockSpec((None, tm, e), lambda bi, si: (bi, si, 0)),
            pl.BlockSpec((None, tm, d), lambda bi, si: (bi, si, 0)),
            _const_spec((e, d)),
            _const_spec((1, d)),
            _const_spec((d, 2 * e)),
        ],
        out_specs=(
            pl.BlockSpec((None, tm, d), lambda bi, si: (bi, si, 0)),
            pl.BlockSpec((tm, e), lambda bi, si: (si, bi)),
            pl.BlockSpec((tm, e), lambda bi, si: (si, bi)),
        ),
        compiler_params=_params("parallel", "parallel"),
        name="l0_out_l1_in",
    )(o, x, wo_bf, norm_g.reshape(1, d), wi_bf)


def _ssm_kernel(u_ref, bt_ref, ct_ref, a_ref, y_ref, st_scr, h_scr, *, batch):
    c = pl.program_id(1)
    n_steps = u_ref.shape[0] // batch

    @pl.when(c == 0)
    def _():
        h_scr[...] = jnp.zeros(h_scr.shape, F32)

    half = u_ref.shape[0] // 2
    for r in range(2):
        rs = slice(r * half, (r + 1) * half)
        st_scr[rs, :] = jnp.dot(u_ref[rs, :], bt_ref[...], preferred_element_type=F32)

    for w in range(SSM_TILE_STATE // SCAN_LANES):
        re = pl.ds(w * SCAN_LANES, SCAN_LANES)
        im = pl.ds(SSM_TILE_STATE + w * SCAN_LANES, SCAN_LANES)
        a_re = jnp.broadcast_to(a_ref[0:1, re], (batch, SCAN_LANES))
        a_im = jnp.broadcast_to(a_ref[1:2, re], (batch, SCAN_LANES))

        def step(t, carry):
            h_re, h_im = carry
            rows = pl.ds(pl.multiple_of(t * batch, batch), batch)
            n_re = a_re * h_re - a_im * h_im + st_scr[rows, re]
            n_im = a_re * h_im + a_im * h_re + st_scr[rows, im]
            st_scr[rows, re] = n_re
            st_scr[rows, im] = n_im
            return n_re, n_im

        h_re, h_im = lax.fori_loop(0, n_steps, step, (h_scr[:, re], h_scr[:, im]), unroll=4)
        h_scr[:, re] = h_re
        h_scr[:, im] = h_im

    for r in range(2):
        rs = slice(r * half, (r + 1) * half)
        y_ref[rs, :] = jnp.dot(st_scr[rs, :].astype(BF16), ct_ref[...],
                               preferred_element_type=F32).astype(BF16)


def _l1_ssm(u_tb, bt, ct, a_tiles, batch):
    rows, e = u_tb.shape
    n_tiles = e // SSM_TILE_CH
    steps = _pick(rows // batch, 128)
    blk = steps * batch
    return pl.pallas_call(
        functools.partial(_ssm_kernel, batch=batch),
        out_shape=jax.ShapeDtypeStruct((rows, e), BF16),
        grid=(n_tiles, rows // blk),
        in_specs=[
            pl.BlockSpec((blk, SSM_TILE_CH), lambda i, c: (c, i)),
            pl.BlockSpec((None, SSM_TILE_CH, 2 * SSM_TILE_STATE), lambda i, c: (i, 0, 0)),
            pl.BlockSpec((None, 2 * SSM_TILE_STATE, SSM_TILE_CH), lambda i, c: (i, 0, 0)),
            pl.BlockSpec((None, 2, SSM_TILE_STATE), lambda i, c: (i, 0, 0)),
        ],
        out_specs=pl.BlockSpec((blk, SSM_TILE_CH), lambda i, c: (c, i)),
        scratch_shapes=[pltpu.VMEM((blk, 2 * SSM_TILE_STATE), F32),
                        pltpu.VMEM((batch, 2 * SSM_TILE_STATE), F32)],
        compiler_params=_params("parallel", "arbitrary"),
        name="l1_ssm",
    )(u_tb, bt, ct, a_tiles)


def _glu_out_kernel(y_ref, u_ref, z_ref, x1_ref, d_ref, wg_ref, bg_ref, wo_ref, o_ref):
    y = y_ref[...].astype(F32) + d_ref[...] * u_ref[...].astype(F32)
    cdf = 0.5 * (1.0 + jnp.tanh(math.sqrt(2.0 / math.pi) * (y + 0.044715 * (y * y * y))))
    y = y * cdf
    gate = jnp.dot(y.astype(BF16), wg_ref[...], preferred_element_type=F32) + bg_ref[...]
    y = y * (1.0 / (1.0 + jnp.exp(-gate)))
    z = z_ref[...].astype(F32)
    y = y * (z / (1.0 + jnp.exp(-z)))
    o_ref[...] = x1_ref[...] + jnp.dot(y.astype(BF16), wo_ref[...], preferred_element_type=F32)


def _l1_glu_out(y_tb, u_tb, z_tb, x1, d_skip, wg_bf, b_glu, wo_bf):
    b, s, d = x1.shape
    e = wg_bf.shape[0]
    ts = _pick(s, 256)
    tb_spec = pl.BlockSpec((ts, e), lambda bi, si: (si, bi))
    return pl.pallas_call(
        _glu_out_kernel,
        out_shape=jax.ShapeDtypeStruct((b, s, d), F32),
        grid=(b, s // ts),
        in_specs=[tb_spec, tb_spec, tb_spec,
                  pl.BlockSpec((None, ts, d), lambda bi, si: (bi, si, 0)),
                  _const_spec((1, e)), _const_spec((e, e)), _const_spec((1, e)),
                  _const_spec((e, d))],
        out_specs=pl.BlockSpec((None, ts, d), lambda bi, si: (bi, si, 0)),
        compiler_params=_params("parallel", "parallel"),
        name="l1_glu_out",
    )(y_tb, u_tb, z_tb, x1, d_skip.reshape(1, e), wg_bf, b_glu.reshape(1, e), wo_bf)


def _s5_tiles(lam_re, lam_im, log_dt, b_re, b_im, c_re, c_im):
    g, p = lam_re.shape
    ch = b_re.shape[2]
    nt = g // SSM_TILE_GROUPS
    dt = jnp.exp(log_dt.astype(F32))[:, None]
    lr, li = lam_re.astype(F32), lam_im.astype(F32)
    mag = jnp.exp(lr * dt)
    ab_re, ab_im = mag * jnp.cos(li * dt), mag * jnp.sin(li * dt)
    den = lr * lr + li * li
    nr, ni = ab_re - 1.0, ab_im
    g_re = (nr * lr + ni * li) / den
    g_im = (ni * lr - nr * li) / den
    br, bi = b_re.astype(F32), b_im.astype(F32)
    bb_re = g_re[..., None] * br - g_im[..., None] * bi
    bb_im = g_re[..., None] * bi + g_im[..., None] * br
    eye = jnp.eye(SSM_TILE_GROUPS, dtype=F32)

    def b_tile(bb):
        t = bb.reshape(nt, SSM_TILE_GROUPS, p, ch)
        t = jnp.einsum("ngpc,gh->ngchp", t, eye)
        return t.reshape(nt, SSM_TILE_GROUPS * ch, SSM_TILE_GROUPS * p)

    def c_tile(cc):
        t = cc.reshape(nt, SSM_TILE_GROUPS, ch, p)
        t = jnp.einsum("ngcp,gh->ngphc", t, eye)
        return t.reshape(nt, SSM_TILE_GROUPS * p, SSM_TILE_GROUPS * ch)

    bt = jnp.concatenate([b_tile(bb_re), b_tile(bb_im)], axis=2).astype(BF16)
    ct = jnp.concatenate([c_tile(c_re.astype(F32)), -c_tile(c_im.astype(F32))], axis=1).astype(BF16)
    a_tiles = jnp.stack([ab_re.reshape(nt, SSM_TILE_STATE), ab_im.reshape(nt, SSM_TILE_STATE)], axis=1)
    return bt, ct, a_tiles


def kernel(x, l0_norm_g, l0_w_in, l0_q_norm_g, l0_k_norm_g, l0_lam_q1, l0_lam_k1, l0_lam_q2, l0_lam_k2, l0_head_norm_g, l0_w_out, l1_norm_g, l1_w_in, l1_lam_re, l1_lam_im, l1_log_dt, l1_b_re, l1_b_im, l1_c_re, l1_c_im, l1_d, l1_w_glu, l1_b_glu, l1_w_out):
    b, s, d = x.shape
    e = l0_w_out.shape[0]
    n_heads = e // HEAD_V
    assert s % LANES == 0 and e % MXU_DIM == 0 and b == SUBLANES

    gq = l0_q_norm_g.astype(F32) * (SUB_HEAD ** -0.5 * LOG2E)
    gk = l0_k_norm_g.astype(F32)
    reps = e // SUB_HEAD
    col_scale = jnp.concatenate([jnp.tile(gq, reps), jnp.tile(gk, reps),
                                 jnp.ones((2 * e,), F32)]).reshape(1, 4 * e)
    qkvz = _l0_inproj(x.reshape(b * s, d), l0_norm_g.astype(F32), l0_w_in.astype(BF16),
                      col_scale, 2 * e).reshape(b, s, 4 * e)

    score_bound = (BF16_ROUNDING_SLACK * SUB_HEAD * jnp.max(jnp.abs(gq)) * jnp.max(jnp.abs(gk))).reshape(1)
    lam_vecs = jnp.stack([l0_lam_q1, l0_lam_k1, l0_lam_q2, l0_lam_k2]).astype(F32)
    o = _l0_attn(qkvz, lam_vecs, l0_head_norm_g.astype(F32), score_bound, n_heads)

    x1, u_tb, z_tb = _l0_out_l1_in(o, x, l0_w_out.astype(BF16), l1_norm_g.astype(F32),
                                   l1_w_in.astype(BF16))

    bt, ct, a_tiles = _s5_tiles(l1_lam_re, l1_lam_im, l1_log_dt, l1_b_re, l1_b_im, l1_c_re, l1_c_im)
    y_tb = _l1_ssm(u_tb.reshape(s * b, e), bt, ct, a_tiles, b).reshape(s, b * e)

    return _l1_glu_out(y_tb, u_tb, z_tb, x1, l1_d.astype(F32), l1_w_glu.astype(BF16),
                       l1_b_glu.astype(F32), l1_w_out.astype(BF16))
```

```python
import functools
import math

import jax
import jax.numpy as jnp
import numpy as np
from jax import lax
from jax.experimental import pallas as pl
from jax.experimental.pallas import tpu as pltpu

F32 = jnp.float32
BF16 = jnp.bfloat16

EPS = 1e-6
SUB_HEAD = 64
HEAD_V = 2 * SUB_HEAD
GROUP_CH = 16
GROUP_STATE = 64
LAMBDA_INIT_L0 = 0.8 - 0.6 * math.exp(-0.3 * 0)
LOG2E = math.log2(math.e)

LANES = 128
SUBLANES = 8
MXU_DIM = 256
VMEM_LIMIT_BYTES = 56 * 1024 * 1024
NEG_BIG = -1e30
F32_MIN_EXP = 126

ATTN_TQ = 512
ATTN_TQ_ONLINE = 256
FIXED_SHIFT_MAX_LOG2 = (F32_MIN_EXP - 26) / 2.0
BF16_ROUNDING_SLACK = 1.01
PAIRS_PER_BODY = 4
DIAG_TILES_PER_BODY = 2

SSM_TILE_CH = MXU_DIM
SSM_TILE_GROUPS = SSM_TILE_CH // GROUP_CH
SSM_TILE_STATE = SSM_TILE_GROUPS * GROUP_STATE
SCAN_LANES = 512


def _params(*sem):
    return pltpu.CompilerParams(dimension_semantics=sem, vmem_limit_bytes=VMEM_LIMIT_BYTES)


def _const_spec(shape):
    nd = len(shape)
    return pl.BlockSpec(shape, lambda *_: (0,) * nd, pipeline_mode=pl.Buffered(1))


def _pick(n, pref):
    t = min(n, pref)
    while n % t:
        t //= 2
    return t


def _l0_inproj_kernel(x_ref, g_ref, w_ref, cs_ref, e_ref, o_ref, h_scr, *, n_norm_tiles):
    j = pl.program_id(1)

    @pl.when(j == 0)
    def _():
        xf = x_ref[...]
        ms = jnp.mean(xf * xf, axis=-1, keepdims=True)
        h_scr[...] = (xf * lax.rsqrt(ms + EPS) * g_ref[...]).astype(BF16)

    acc = jnp.dot(h_scr[...], w_ref[...], preferred_element_type=F32)
    tn = acc.shape[1]

    @pl.when(j < n_norm_tiles)
    def _():
        sq = acc * acc
        hi = sq.astype(BF16)
        lo = (sq - hi.astype(F32)).astype(BF16)
        e = e_ref[...]
        for c in range(tn // MXU_DIM):
            sl = slice(c * MXU_DIM, (c + 1) * MXU_DIM)
            ss = (jnp.dot(hi[:, sl], e, preferred_element_type=F32)
                  + jnp.dot(lo[:, sl], e, preferred_element_type=F32))
            inv = lax.rsqrt(ss * (1.0 / SUB_HEAD) + EPS)
            o_ref[:, sl] = (acc[:, sl] * inv * cs_ref[:, sl]).astype(BF16)

    @pl.when(j >= n_norm_tiles)
    def _():
        o_ref[...] = acc.astype(BF16)


def _l0_inproj(x2d, norm_g, w_bf, col_scale, n_norm_cols):
    m, d = x2d.shape
    n = w_bf.shape[1]
    tm = _pick(m, 1024)
    tn = _pick(n_norm_cols, 512)
    seg = np.arange(MXU_DIM) // SUB_HEAD
    e = jnp.asarray(seg[:, None] == seg[None, :], dtype=BF16)
    return pl.pallas_call(
        functools.partial(_l0_inproj_kernel, n_norm_tiles=n_norm_cols // tn),
        out_shape=jax.ShapeDtypeStruct((m, n), BF16),
        grid=(m // tm, n // tn),
        in_specs=[
            pl.BlockSpec((tm, d), lambda i, j: (i, 0)),
            _const_spec((1, d)),
            pl.BlockSpec((d, tn), lambda i, j: (0, j)),
            pl.BlockSpec((1, tn), lambda i, j: (0, j)),
            _const_spec((MXU_DIM, MXU_DIM)),
        ],
        out_specs=pl.BlockSpec((tm, tn), lambda i, j: (i, j)),
        scratch_shapes=[pltpu.VMEM((tm, d), BF16)],
        compiler_params=_params("parallel", "arbitrary"),
        name="l0_inproj",
    )(x2d, norm_g.reshape(1, d), w_bf, col_scale, e)


def _split3(val):
    p1 = val.astype(BF16).astype(F32)
    r1 = val - p1
    p2 = r1.astype(BF16).astype(F32)
    p3 = (r1 - p2).astype(BF16).astype(F32)
    return p1, p2, p3


def _augment(parts, lane, off, ones_first):
    p_off = off + 3 if ones_first else off
    o_off = off if ones_first else off + 3
    a = jnp.where(lane == p_off, parts[0], 0.0)
    a = jnp.where(lane == p_off + 1, parts[1], a)
    a = jnp.where(lane == p_off + 2, parts[2], a)
    for t in range(3):
        a = jnp.where(lane == o_off + t, 1.0, a)
    return a


def _grouped_loop(n, group, fn):
    def body(t, carry):
        for g in range(group):
            fn(t * group + g)
        return carry

    lax.fori_loop(0, n // group, body, 0)
    for r in range(n - n % group, n):
        fn(r)


def _scores(qa, kk):
    return lax.dot_general(qa, kk, (((1,), (1,)), ((), ())), preferred_element_type=F32)


def _attn_kernel(pairs_ref, shift_ref, q_ref, k_ref, v_ref, z_ref, lam_ref, hg_ref, slope_ref, o_ref,
                 aq_scr, ak_scr, qaug_scr, kaug_scr, vaug_scr, acc_scr, acc1_scr, m_scr, l_scr,
                 *, tq, tq1, n_pairs):
    s_len = q_ref.shape[0]
    n_blk = s_len // tq
    half = tq // 2
    h = pl.program_id(0)
    shift = shift_ref[0]
    slope = slope_ref[pl.ds(h, 1), :] * LOG2E
    lane = lax.broadcasted_iota(jnp.int32, (tq, LANES), 1)
    row = lax.broadcasted_iota(jnp.int32, (tq, LANES), 0)

    lam_v = lam_ref[...]
    lam = (jnp.exp(jnp.sum(lam_v[0:1] * lam_v[1:2], axis=1, keepdims=True))
           - jnp.exp(jnp.sum(lam_v[2:3] * lam_v[3:4], axis=1, keepdims=True)) + LAMBDA_INIT_L0)

    def blk(i, size):
        return pl.ds(pl.multiple_of(i * size, size), size)

    def finish(rows, o0, l0, o1, l1):
        o = o0 * (1.0 / l0) - lam * (o1 * (1.0 / l1))
        ms = jnp.mean(o * o, axis=-1, keepdims=True)
        on = o * lax.rsqrt(ms + EPS) * hg_ref[...] * (1.0 - LAMBDA_INIT_L0)
        z = z_ref[rows, :].astype(F32)
        o_ref[rows, :] = (on * (z / (1.0 + jnp.exp(-z)))).astype(BF16)

    @pl.when(pl.program_id(1) == 0)
    def _tables():
        ones_col = jnp.ones((tq, HEAD_V), BF16)

        def body(c, carry):
            rows = blk(c, tq)
            pos = slope * (c * tq + row).astype(F32)
            k_parts = _split3(pos)
            q_parts = _split3(-(pos + shift))
            ak_scr[0, rows, :] = _augment(k_parts, lane, SUB_HEAD, True).astype(BF16)
            ak_scr[1, rows, :] = _augment(k_parts, lane, 0, True).astype(BF16)
            aq_scr[0, rows, :] = _augment(q_parts, lane, SUB_HEAD, False).astype(BF16)
            aq_scr[1, rows, :] = _augment(q_parts, lane, 0, False).astype(BF16)
            vaug_scr[rows, HEAD_V:2 * HEAD_V] = ones_col
            return carry

        lax.fori_loop(0, n_blk, body, 0)

    lane_row = lax.broadcasted_iota(jnp.int32, (1, LANES), 1)
    keep = (jnp.where(lane_row < SUB_HEAD, 1.0, 0.0).astype(BF16),
            jnp.where(lane_row < SUB_HEAD, 0.0, 1.0).astype(BF16))

    def build(c, carry):
        rows = blk(c, tq)
        kb = k_ref[rows, :]
        qb = q_ref[rows, :]
        for sub in range(2):
            kaug_scr[sub, rows, :] = kb * keep[sub] + ak_scr[sub, rows, :]
            qaug_scr[sub, rows, :] = qb * keep[sub] + aq_scr[sub, rows, :]
        vaug_scr[rows, 0:HEAD_V] = v_ref[rows, :]
        return carry

    lax.fori_loop(0, n_blk, build, 0)

    fixed_shift_ok = shift <= FIXED_SHIFT_MAX_LOG2

    @pl.when(fixed_shift_ok)
    def _fixed_shift():
        def tile(q_rows, k_rows, mask_off, init):
            n_q, n_k = q_rows.size, k_rows.size
            for sub in range(2):
                s = _scores(qaug_scr[sub, q_rows, :], kaug_scr[sub, k_rows, :])
                if mask_off is not None:
                    cols = lax.broadcasted_iota(jnp.int32, (n_q, n_k), 1)
                    rws = lax.broadcasted_iota(jnp.int32, (n_q, n_k), 0)
                    s = jnp.where(cols <= rws + mask_off, s, -jnp.inf)
                pv = jnp.dot(jnp.exp2(s).astype(BF16), vaug_scr[k_rows, :], preferred_element_type=F32)
                if init:
                    acc_scr[sub, q_rows, :] = pv
                else:
                    acc_scr[sub, q_rows, :] += pv

        def diagonal(i):
            r0 = pl.multiple_of(i * tq, tq)
            tile(pl.ds(r0, half), pl.ds(r0, half), 0, True)
            tile(pl.ds(r0 + half, half), pl.ds(r0, tq), half, True)

        def off_diagonal(p):
            tile(blk(pairs_ref[2 * p], tq), blk(pairs_ref[2 * p + 1], tq), None, False)

        _grouped_loop(n_blk, DIAG_TILES_PER_BODY, diagonal)
        _grouped_loop(n_pairs, PAIRS_PER_BODY, off_diagonal)

        def finalize(i, carry):
            rows = blk(i, tq)
            a0 = acc_scr[0, rows, :]
            a1 = acc_scr[1, rows, :]
            finish(rows, a0[:, :HEAD_V], a0[:, HEAD_V:], a1[:, :HEAD_V], a1[:, HEAD_V:])
            return carry

        lax.fori_loop(0, n_blk, finalize, 0)

    @pl.when(jnp.logical_not(fixed_shift_ok))
    def _online():
        col_s = lax.broadcasted_iota(jnp.int32, (tq1, tq1), 1)
        row_s = lax.broadcasted_iota(jnp.int32, (tq1, tq1), 0)
        causal = col_s <= row_s

        def q_block(i, carry):
            rows = blk(i, tq1)
            m_scr[...] = jnp.full(m_scr.shape, NEG_BIG, F32)
            l_scr[...] = jnp.zeros(l_scr.shape, F32)
            acc1_scr[...] = jnp.zeros(acc1_scr.shape, F32)

            def kv_step(j, masked):
                k_rows = blk(j, tq1)
                v = v_ref[k_rows, :]
                for sub in range(2):
                    s = _scores(qaug_scr[sub, rows, :], kaug_scr[sub, k_rows, :])
                    if masked:
                        s = jnp.where(causal, s, -jnp.inf)
                    m_prev = m_scr[sub]
                    m_new = jnp.maximum(m_prev, jnp.max(s, axis=1, keepdims=True))
                    alpha = jnp.exp2(m_prev - m_new)
                    p = jnp.exp2(s - m_new)
                    l_scr[sub] = alpha * l_scr[sub] + jnp.sum(p, axis=1, keepdims=True)
                    acc1_scr[sub] = alpha * acc1_scr[sub] + jnp.dot(
                        p.astype(BF16), v, preferred_element_type=F32)
                    m_scr[sub] = m_new

            def off_diag(j, c):
                kv_step(j, False)
                return c

            lax.fori_loop(0, i, off_diag, 0)
            kv_step(i, True)
            finish(rows, acc1_scr[0], l_scr[0], acc1_scr[1], l_scr[1])
            return carry

        lax.fori_loop(0, s_len // tq1, q_block, 0)


def _l0_attn(qkvz, lam_vecs, head_g, score_bound, n_heads):
    b, s, _ = qkvz.shape
    tq = _pick(s, ATTN_TQ)
    tq1 = _pick(s, ATTN_TQ_ONLINE)
    n_blk = s // tq
    slopes = 2.0 ** (-8.0 * np.arange(1, n_heads + 1) / n_heads)
    slope_tab = jnp.asarray(np.broadcast_to(slopes[:, None], (n_heads, LANES)), dtype=F32)
    pairs = np.asarray([(i, j) for j in range(n_blk) for i in range(j + 1, n_blk)],
                       dtype=np.int32).reshape(-1)
    n_pairs = pairs.size // 2
    pairs = jnp.asarray(np.concatenate([pairs, np.zeros(2, np.int32)]))

    def head_block(which):
        return pl.BlockSpec((None, s, HEAD_V), lambda hi, bi: (bi, 0, which * n_heads + hi))

    smem = pl.BlockSpec(memory_space=pltpu.SMEM)
    return pl.pallas_call(
        functools.partial(_attn_kernel, tq=tq, tq1=tq1, n_pairs=n_pairs),
        out_shape=jax.ShapeDtypeStruct((b, s, n_heads * HEAD_V), BF16),
        grid=(n_heads, b),
        in_specs=[smem, smem, head_block(0), head_block(1), head_block(2), head_block(3),
                  _const_spec((4, SUB_HEAD)), _const_spec((1, HEAD_V)),
                  _const_spec((n_heads, LANES))],
        out_specs=pl.BlockSpec((None, s, HEAD_V), lambda hi, bi: (bi, 0, hi)),
        scratch_shapes=[pltpu.VMEM((2, s, HEAD_V), BF16),
                        pltpu.VMEM((2, s, HEAD_V), BF16),
                        pltpu.VMEM((2, s, HEAD_V), BF16),
                        pltpu.VMEM((2, s, HEAD_V), BF16),
                        pltpu.VMEM((s, 2 * HEAD_V), BF16),
                        pltpu.VMEM((2, s, 2 * HEAD_V), F32),
                        pltpu.VMEM((2, tq1, HEAD_V), F32),
                        pltpu.VMEM((2, tq1, 1), F32),
                        pltpu.VMEM((2, tq1, 1), F32)],
        compiler_params=_params("parallel", "arbitrary"),
        name="l0_attn",
    )(pairs, score_bound, qkvz, qkvz, qkvz, qkvz, lam_vecs, head_g.reshape(1, HEAD_V), slope_tab)


def _l0_out_l1_in_kernel(o_ref, x_ref, wo_ref, g_ref, wi_ref, x1_ref, u_ref, z_ref):
    e = u_ref.shape[1]
    x1 = x_ref[...] + jnp.dot(o_ref[...], wo_ref[...], preferred_element_type=F32)
    x1_ref[...] = x1
    ms = jnp.mean(x1 * x1, axis=-1, keepdims=True)
    hn = (x1 * lax.rsqrt(ms + EPS) * g_ref[...]).astype(BF16)
    u_ref[...] = jnp.dot(hn, wi_ref[:, :e], preferred_element_type=F32).astype(BF16)
    z_ref[...] = jnp.dot(hn, wi_ref[:, e:], preferred_element_type=F32).astype(BF16)


def _l0_out_l1_in(o, x, wo_bf, norm_g, wi_bf):
    b, s, d = x.shape
    e = o.shape[2]
    tm = _pick(s, 512)
    act_shape = jax.ShapeDtypeStruct((b, s, e), BF16)
    return pl.pallas_call(
        _l0_out_l1_in_kernel,
        out_shape=(jax.ShapeDtypeStruct((b, s, d), F32), act_shape, act_shape),
        grid=(b, s // tm),
        in_specs=[
            pl.BlockSpec((None, tm, e), lambda bi, si: (bi, si, 0)),
            pl.BlockSpec((None, tm, d), lambda bi, si: (bi, si, 0)),
            _const_spec((e, d)),
            _const_spec((1, d)),
            _const_spec((d, 2 * e)),
        ],
        out_specs=(
            pl.BlockSpec((None, tm, d), lambda bi, si: (bi, si, 0)),
            pl.BlockSpec((None, tm, e), lambda bi, si: (bi, si, 0)),
            pl.BlockSpec((None, tm, e), lambda bi, si: (bi, si, 0)),
        ),
        compiler_params=_params("parallel", "parallel"),
        name="l0_out_l1_in",
    )(o, x, wo_bf, norm_g.reshape(1, d), wi_bf)


def _ssm_kernel(u_ref, bt_ref, ct_ref, a_ref, y_ref, il_scr, st_scr, h_scr):
    batch, n_steps, _ = u_ref.shape
    n_slabs = il_scr.shape[0]
    c = pl.program_id(1)

    @pl.when(c == 0)
    def _():
        h_scr[...] = jnp.zeros(h_scr.shape, F32)

    for b in range(batch):
        ub = u_ref[b].astype(F32)
        for k in range(n_slabs):
            il_scr[k, pl.ds(b, n_steps, stride=batch), :] = ub[:, k * LANES:(k + 1) * LANES]

    half = batch * n_steps // 2
    for r in range(2):
        rs = slice(r * half, (r + 1) * half)
        u_il = jnp.concatenate([il_scr[k, rs, :] for k in range(n_slabs)], axis=1).astype(BF16)
        st_scr[rs, :] = jnp.dot(u_il, bt_ref[...], preferred_element_type=F32)

    for w in range(SSM_TILE_STATE // SCAN_LANES):
        re = pl.ds(w * SCAN_LANES, SCAN_LANES)
        im = pl.ds(SSM_TILE_STATE + w * SCAN_LANES, SCAN_LANES)
        a_re = jnp.broadcast_to(a_ref[0:1, re], (batch, SCAN_LANES))
        a_im = jnp.broadcast_to(a_ref[1:2, re], (batch, SCAN_LANES))

        def step(t, carry):
            h_re, h_im = carry
            rows = pl.ds(pl.multiple_of(t * batch, batch), batch)
            n_re = a_re * h_re - a_im * h_im + st_scr[rows, re]
            n_im = a_re * h_im + a_im * h_re + st_scr[rows, im]
            st_scr[rows, re] = n_re
            st_scr[rows, im] = n_im
            return n_re, n_im

        h_re, h_im = lax.fori_loop(0, n_steps, step, (h_scr[:, re], h_scr[:, im]), unroll=4)
        h_scr[:, re] = h_re
        h_scr[:, im] = h_im

    for r in range(2):
        rs = slice(r * half, (r + 1) * half)
        y = jnp.dot(st_scr[rs, :].astype(BF16), ct_ref[...], preferred_element_type=F32)
        for k in range(n_slabs):
            il_scr[k, rs, :] = y[:, k * LANES:(k + 1) * LANES]
    for b in range(batch):
        for k in range(n_slabs):
            y_ref[b, :, k * LANES:(k + 1) * LANES] = il_scr[
                k, pl.ds(b, n_steps, stride=batch), :].astype(BF16)


def _l1_ssm(u, bt, ct, a_tiles):
    batch, s, e = u.shape
    n_tiles = e // SSM_TILE_CH
    steps = _pick(s, 128)
    rows = steps * batch
    act_spec = pl.BlockSpec((batch, steps, SSM_TILE_CH), lambda i, c: (0, c, i))
    return pl.pallas_call(
        _ssm_kernel,
        out_shape=jax.ShapeDtypeStruct((batch, s, e), BF16),
        grid=(n_tiles, s // steps),
        in_specs=[
            act_spec,
            pl.BlockSpec((None, SSM_TILE_CH, 2 * SSM_TILE_STATE), lambda i, c: (i, 0, 0)),
            pl.BlockSpec((None, 2 * SSM_TILE_STATE, SSM_TILE_CH), lambda i, c: (i, 0, 0)),
            pl.BlockSpec((None, 2, SSM_TILE_STATE), lambda i, c: (i, 0, 0)),
        ],
        out_specs=act_spec,
        scratch_shapes=[pltpu.VMEM((SSM_TILE_CH // LANES, rows, LANES), F32),
                        pltpu.VMEM((rows, 2 * SSM_TILE_STATE), F32),
                        pltpu.VMEM((batch, 2 * SSM_TILE_STATE), F32)],
        compiler_params=_params("parallel", "arbitrary"),
        name="l1_ssm",
    )(u, bt, ct, a_tiles)


def _glu_out_kernel(y_ref, u_ref, z_ref, x1_ref, d_ref, wg_ref, bg_ref, wo_ref, o_ref):
    y = y_ref[...].astype(F32) + d_ref[...] * u_ref[...].astype(F32)
    cdf = 0.5 * (1.0 + jnp.tanh(math.sqrt(2.0 / math.pi) * (y + 0.044715 * (y * y * y))))
    y = y * cdf
    gate = jnp.dot(y.astype(BF16), wg_ref[...], preferred_element_type=F32) + bg_ref[...]
    y = y * (1.0 / (1.0 + jnp.exp(-gate)))
    z = z_ref[...].astype(F32)
    y = y * (z / (1.0 + jnp.exp(-z)))
    o_ref[...] = x1_ref[...] + jnp.dot(y.astype(BF16), wo_ref[...], preferred_element_type=F32)


def _l1_glu_out(y, u, z, x1, d_skip, wg_bf, b_glu, wo_bf):
    b, s, d = x1.shape
    e = wg_bf.shape[0]
    ts = _pick(s, 256)
    act_spec = pl.BlockSpec((None, ts, e), lambda bi, si: (bi, si, 0))
    return pl.pallas_call(
        _glu_out_kernel,
        out_shape=jax.ShapeDtypeStruct((b, s, d), F32),
        grid=(b, s // ts),
        in_specs=[act_spec, act_spec, act_spec,
                  pl.BlockSpec((None, ts, d), lambda bi, si: (bi, si, 0)),
                  _const_spec((1, e)), _const_spec((e, e)), _const_spec((1, e)),
                  _const_spec((e, d))],
        out_specs=pl.BlockSpec((None, ts, d), lambda bi, si: (bi, si, 0)),
        compiler_params=_params("parallel", "parallel"),
        name="l1_glu_out",
    )(y, u, z, x1, d_skip.reshape(1, e), wg_bf, b_glu.reshape(1, e), wo_bf)


def _s5_tiles(lam_re, lam_im, log_dt, b_re, b_im, c_re, c_im):
    g, p = lam_re.shape
    ch = b_re.shape[2]
    nt = g // SSM_TILE_GROUPS
    dt = jnp.exp(log_dt.astype(F32))[:, None]
    lr, li = lam_re.astype(F32), lam_im.astype(F32)
    mag = jnp.exp(lr * dt)
    ab_re, ab_im = mag * jnp.cos(li * dt), mag * jnp.sin(li * dt)
    den = lr * lr + li * li
    nr, ni = ab_re - 1.0, ab_im
    g_re = (nr * lr + ni * li) / den
    g_im = (ni * lr - nr * li) / den
    br, bi = b_re.astype(F32), b_im.astype(F32)
    bb_re = g_re[..., None] * br - g_im[..., None] * bi
    bb_im = g_re[..., None] * bi + g_im[..., None] * br
    eye = jnp.eye(SSM_TILE_GROUPS, dtype=F32)

    def b_tile(bb):
        t = bb.reshape(nt, SSM_TILE_GROUPS, p, ch)
        t = jnp.einsum("ngpc,gh->ngchp", t, eye)
        return t.reshape(nt, SSM_TILE_GROUPS * ch, SSM_TILE_GROUPS * p)

    def c_tile(cc):
        t = cc.reshape(nt, SSM_TILE_GROUPS, ch, p)
        t = jnp.einsum("ngcp,gh->ngphc", t, eye)
        return t.reshape(nt, SSM_TILE_GROUPS * p, SSM_TILE_GROUPS * ch)

    bt = jnp.concatenate([b_tile(bb_re), b_tile(bb_im)], axis=2).astype(BF16)
    ct = jnp.concatenate([c_tile(c_re.astype(F32)), -c_tile(c_im.astype(F32))], axis=1).astype(BF16)
    a_tiles = jnp.stack([ab_re.reshape(nt, SSM_TILE_STATE), ab_im.reshape(nt, SSM_TILE_STATE)], axis=1)
    return bt, ct, a_tiles


def kernel(x, l0_norm_g, l0_w_in, l0_q_norm_g, l0_k_norm_g, l0_lam_q1, l0_lam_k1, l0_lam_q2, l0_lam_k2, l0_head_norm_g, l0_w_out, l1_norm_g, l1_w_in, l1_lam_re, l1_lam_im, l1_log_dt, l1_b_re, l1_b_im, l1_c_re, l1_c_im, l1_d, l1_w_glu, l1_b_glu, l1_w_out):
    b, s, d = x.shape
    e = l0_w_out.shape[0]
    n_heads = e // HEAD_V
    assert s % LANES == 0 and e % MXU_DIM == 0 and b == SUBLANES

    gq = l0_q_norm_g.astype(F32) * (SUB_HEAD ** -0.5 * LOG2E)
    gk = l0_k_norm_g.astype(F32)
    reps = e // SUB_HEAD
    col_scale = jnp.concatenate([jnp.tile(gq, reps), jnp.tile(gk, reps),
                                 jnp.ones((2 * e,), F32)]).reshape(1, 4 * e)
    qkvz = _l0_inproj(x.reshape(b * s, d), l0_norm_g.astype(F32), l0_w_in.astype(BF16),
                      col_scale, 2 * e).reshape(b, s, 4 * e)

    score_bound = (BF16_ROUNDING_SLACK * SUB_HEAD * jnp.max(jnp.abs(gq)) * jnp.max(jnp.abs(gk))).reshape(1)
    lam_vecs = jnp.stack([l0_lam_q1, l0_lam_k1, l0_lam_q2, l0_lam_k2]).astype(F32)
    o = _l0_attn(qkvz, lam_vecs, l0_head_norm_g.astype(F32), score_bound, n_heads)

    x1, u, z = _l0_out_l1_in(o, x, l0_w_out.astype(BF16), l1_norm_g.astype(F32),
                             l1_w_in.astype(BF16))

    bt, ct, a_tiles = _s5_tiles(l1_lam_re, l1_lam_im, l1_log_dt, l1_b_re, l1_b_im, l1_c_re, l1_c_im)
    y = _l1_ssm(u, bt, ct, a_tiles)

    return _l1_glu_out(y, u, z, x1, l1_d.astype(F32), l1_w_glu.astype(BF16),
                       l1_b_glu.astype(F32), l1_w_out.astype(BF16))
```

```python
import functools
import math

import jax
import jax.numpy as jnp
import numpy as np
from jax import lax
from jax.experimental import pallas as pl
from jax.experimental.pallas import tpu as pltpu

F32 = jnp.float32
BF16 = jnp.bfloat16

EPS = 1e-6
SUB_HEAD = 64
HEAD_V = 2 * SUB_HEAD
GROUP_CH = 16
GROUP_STATE = 64
LAMBDA_INIT_L0 = 0.8 - 0.6 * math.exp(-0.3 * 0)
LOG2E = math.log2(math.e)

LANES = 128
SUBLANES = 8
MXU_DIM = 256
VMEM_LIMIT_BYTES = 56 * 1024 * 1024
NEG_BIG = -1e30
F32_MIN_EXP = 126

ATTN_TQ = 512
ATTN_TQ_ONLINE = 256
FIXED_SHIFT_MAX_LOG2 = (F32_MIN_EXP - 26) / 2.0
BF16_ROUNDING_SLACK = 1.01
PAIRS_PER_BODY = 4
DIAG_TILES_PER_BODY = 2

SSM_TILE_CH = MXU_DIM
SSM_TILE_GROUPS = SSM_TILE_CH // GROUP_CH
SSM_TILE_STATE = SSM_TILE_GROUPS * GROUP_STATE
SCAN_LANES = 512
SSM_STAGES = 3


def _params(*sem):
    return pltpu.CompilerParams(dimension_semantics=sem, vmem_limit_bytes=VMEM_LIMIT_BYTES)


def _const_spec(shape):
    nd = len(shape)
    return pl.BlockSpec(shape, lambda *_: (0,) * nd, pipeline_mode=pl.Buffered(1))


def _pick(n, pref):
    t = min(n, pref)
    while n % t:
        t //= 2
    return t


def _l0_inproj_kernel(x_ref, g_ref, w_ref, cs_ref, e_ref, o_ref, h_scr, *, n_norm_tiles):
    j = pl.program_id(1)

    @pl.when(j == 0)
    def _():
        xf = x_ref[...]
        ms = jnp.mean(xf * xf, axis=-1, keepdims=True)
        h_scr[...] = (xf * lax.rsqrt(ms + EPS) * g_ref[...]).astype(BF16)

    acc = jnp.dot(h_scr[...], w_ref[...], preferred_element_type=F32)
    tn = acc.shape[1]

    @pl.when(j < n_norm_tiles)
    def _():
        sq = acc * acc
        hi = sq.astype(BF16)
        lo = (sq - hi.astype(F32)).astype(BF16)
        e = e_ref[...]
        for c in range(tn // MXU_DIM):
            sl = slice(c * MXU_DIM, (c + 1) * MXU_DIM)
            ss = (jnp.dot(hi[:, sl], e, preferred_element_type=F32)
                  + jnp.dot(lo[:, sl], e, preferred_element_type=F32))
            inv = lax.rsqrt(ss * (1.0 / SUB_HEAD) + EPS)
            o_ref[:, sl] = (acc[:, sl] * inv * cs_ref[:, sl]).astype(BF16)

    @pl.when(j >= n_norm_tiles)
    def _():
        o_ref[...] = acc.astype(BF16)


def _l0_inproj(x2d, norm_g, w_bf, col_scale, n_norm_cols):
    m, d = x2d.shape
    n = w_bf.shape[1]
    tm = _pick(m, 1024)
    tn = _pick(n_norm_cols, 512)
    seg = np.arange(MXU_DIM) // SUB_HEAD
    e = jnp.asarray(seg[:, None] == seg[None, :], dtype=BF16)
    return pl.pallas_call(
        functools.partial(_l0_inproj_kernel, n_norm_tiles=n_norm_cols // tn),
        out_shape=jax.ShapeDtypeStruct((m, n), BF16),
        grid=(m // tm, n // tn),
        in_specs=[
            pl.BlockSpec((tm, d), lambda i, j: (i, 0)),
            _const_spec((1, d)),
            pl.BlockSpec((d, tn), lambda i, j: (0, j)),
            pl.BlockSpec((1, tn), lambda i, j: (0, j)),
            _const_spec((MXU_DIM, MXU_DIM)),
        ],
        out_specs=pl.BlockSpec((tm, tn), lambda i, j: (i, j)),
        scratch_shapes=[pltpu.VMEM((tm, d), BF16)],
        compiler_params=_params("parallel", "arbitrary"),
        name="l0_inproj",
    )(x2d, norm_g.reshape(1, d), w_bf, col_scale, e)


def _split3(val):
    p1 = val.astype(BF16).astype(F32)
    r1 = val - p1
    p2 = r1.astype(BF16).astype(F32)
    p3 = (r1 - p2).astype(BF16).astype(F32)
    return p1, p2, p3


def _augment(parts, lane, off, ones_first):
    p_off = off + 3 if ones_first else off
    o_off = off if ones_first else off + 3
    a = jnp.where(lane == p_off, parts[0], 0.0)
    a = jnp.where(lane == p_off + 1, parts[1], a)
    a = jnp.where(lane == p_off + 2, parts[2], a)
    for t in range(3):
        a = jnp.where(lane == o_off + t, 1.0, a)
    return a


def _grouped_loop(n, group, fn):
    def body(t, carry):
        for g in range(group):
            fn(t * group + g)
        return carry

    lax.fori_loop(0, n // group, body, 0)
    for r in range(n - n % group, n):
        fn(r)


def _scores(qa, kk):
    return lax.dot_general(qa, kk, (((1,), (1,)), ((), ())), preferred_element_type=F32)


def _attn_kernel(pairs_ref, shift_ref, q_ref, k_ref, v_ref, z_ref, lam_ref, hg_ref, slope_ref, o_ref,
                 aq_scr, ak_scr, qaug_scr, kaug_scr, vaug_scr, acc_scr, acc1_scr, m_scr, l_scr,
                 *, tq, tq1, n_pairs):
    s_len = q_ref.shape[0]
    n_blk = s_len // tq
    half = tq // 2
    h = pl.program_id(0)
    shift = shift_ref[0]
    slope = slope_ref[pl.ds(h, 1), :] * LOG2E
    lane = lax.broadcasted_iota(jnp.int32, (tq, LANES), 1)
    row = lax.broadcasted_iota(jnp.int32, (tq, LANES), 0)

    lam_v = lam_ref[...]
    lam = (jnp.exp(jnp.sum(lam_v[0:1] * lam_v[1:2], axis=1, keepdims=True))
           - jnp.exp(jnp.sum(lam_v[2:3] * lam_v[3:4], axis=1, keepdims=True)) + LAMBDA_INIT_L0)

    def blk(i, size):
        return pl.ds(pl.multiple_of(i * size, size), size)

    def finish(rows, o0, l0, o1, l1):
        o = o0 * (1.0 / l0) - lam * (o1 * (1.0 / l1))
        ms = jnp.mean(o * o, axis=-1, keepdims=True)
        on = o * lax.rsqrt(ms + EPS) * hg_ref[...] * (1.0 - LAMBDA_INIT_L0)
        z = z_ref[rows, :].astype(F32)
        o_ref[rows, :] = (on * (z / (1.0 + jnp.exp(-z)))).astype(BF16)

    @pl.when(pl.program_id(1) == 0)
    def _tables():
        ones_col = jnp.ones((tq, HEAD_V), BF16)

        def body(c, carry):
            rows = blk(c, tq)
            pos = slope * (c * tq + row).astype(F32)
            k_parts = _split3(pos)
            q_parts = _split3(-(pos + shift))
            ak_scr[0, rows, :] = _augment(k_parts, lane, SUB_HEAD, True).astype(BF16)
            ak_scr[1, rows, :] = _augment(k_parts, lane, 0, True).astype(BF16)
            aq_scr[0, rows, :] = _augment(q_parts, lane, SUB_HEAD, False).astype(BF16)
            aq_scr[1, rows, :] = _augment(q_parts, lane, 0, False).astype(BF16)
            vaug_scr[rows, HEAD_V:2 * HEAD_V] = ones_col
            return carry

        lax.fori_loop(0, n_blk, body, 0)

    lane_row = lax.broadcasted_iota(jnp.int32, (1, LANES), 1)
    keep = (jnp.where(lane_row < SUB_HEAD, 1.0, 0.0).astype(BF16),
            jnp.where(lane_row < SUB_HEAD, 0.0, 1.0).astype(BF16))

    def build(c, carry):
        rows = blk(c, tq)
        kb = k_ref[rows, :]
        qb = q_ref[rows, :]
        for sub in range(2):
            kaug_scr[sub, rows, :] = kb * keep[sub] + ak_scr[sub, rows, :]
            qaug_scr[sub, rows, :] = qb * keep[sub] + aq_scr[sub, rows, :]
        vaug_scr[rows, 0:HEAD_V] = v_ref[rows, :]
        return carry

    lax.fori_loop(0, n_blk, build, 0)

    fixed_shift_ok = shift <= FIXED_SHIFT_MAX_LOG2

    @pl.when(fixed_shift_ok)
    def _fixed_shift():
        def tile(q_rows, k_rows, mask_off, init):
            n_q, n_k = q_rows.size, k_rows.size
            for sub in range(2):
                s = _scores(qaug_scr[sub, q_rows, :], kaug_scr[sub, k_rows, :])
                if mask_off is not None:
                    cols = lax.broadcasted_iota(jnp.int32, (n_q, n_k), 1)
                    rws = lax.broadcasted_iota(jnp.int32, (n_q, n_k), 0)
                    s = jnp.where(cols <= rws + mask_off, s, -jnp.inf)
                pv = jnp.dot(jnp.exp2(s).astype(BF16), vaug_scr[k_rows, :], preferred_element_type=F32)
                if init:
                    acc_scr[sub, q_rows, :] = pv
                else:
                    acc_scr[sub, q_rows, :] += pv

        def diagonal(i):
            r0 = pl.multiple_of(i * tq, tq)
            tile(pl.ds(r0, half), pl.ds(r0, half), 0, True)
            tile(pl.ds(r0 + half, half), pl.ds(r0, tq), half, True)

        def off_diagonal(p):
            tile(blk(pairs_ref[2 * p], tq), blk(pairs_ref[2 * p + 1], tq), None, False)

        _grouped_loop(n_blk, DIAG_TILES_PER_BODY, diagonal)
        _grouped_loop(n_pairs, PAIRS_PER_BODY, off_diagonal)

        def finalize(i, carry):
            rows = blk(i, tq)
            a0 = acc_scr[0, rows, :]
            a1 = acc_scr[1, rows, :]
            finish(rows, a0[:, :HEAD_V], a0[:, HEAD_V:], a1[:, :HEAD_V], a1[:, HEAD_V:])
            return carry

        lax.fori_loop(0, n_blk, finalize, 0)

    @pl.when(jnp.logical_not(fixed_shift_ok))
    def _online():
        col_s = lax.broadcasted_iota(jnp.int32, (tq1, tq1), 1)
        row_s = lax.broadcasted_iota(jnp.int32, (tq1, tq1), 0)
        causal = col_s <= row_s

        def q_block(i, carry):
            rows = blk(i, tq1)
            m_scr[...] = jnp.full(m_scr.shape, NEG_BIG, F32)
            l_scr[...] = jnp.zeros(l_scr.shape, F32)
            acc1_scr[...] = jnp.zeros(acc1_scr.shape, F32)

            def kv_step(j, masked):
                k_rows = blk(j, tq1)
                v = v_ref[k_rows, :]
                for sub in range(2):
                    s = _scores(qaug_scr[sub, rows, :], kaug_scr[sub, k_rows, :])
                    if masked:
                        s = jnp.where(causal, s, -jnp.inf)
                    m_prev = m_scr[sub]
                    m_new = jnp.maximum(m_prev, jnp.max(s, axis=1, keepdims=True))
                    alpha = jnp.exp2(m_prev - m_new)
                    p = jnp.exp2(s - m_new)
                    l_scr[sub] = alpha * l_scr[sub] + jnp.sum(p, axis=1, keepdims=True)
                    acc1_scr[sub] = alpha * acc1_scr[sub] + jnp.dot(
                        p.astype(BF16), v, preferred_element_type=F32)
                    m_scr[sub] = m_new

            def off_diag(j, c):
                kv_step(j, False)
                return c

            lax.fori_loop(0, i, off_diag, 0)
            kv_step(i, True)
            finish(rows, acc1_scr[0], l_scr[0], acc1_scr[1], l_scr[1])
            return carry

        lax.fori_loop(0, s_len // tq1, q_block, 0)


def _l0_attn(qkvz, lam_vecs, head_g, score_bound, n_heads):
    b, s, _ = qkvz.shape
    tq = _pick(s, ATTN_TQ)
    tq1 = _pick(s, ATTN_TQ_ONLINE)
    n_blk = s // tq
    slopes = 2.0 ** (-8.0 * np.arange(1, n_heads + 1) / n_heads)
    slope_tab = jnp.asarray(np.broadcast_to(slopes[:, None], (n_heads, LANES)), dtype=F32)
    pairs = np.asarray([(i, j) for j in range(n_blk) for i in range(j + 1, n_blk)],
                       dtype=np.int32).reshape(-1)
    n_pairs = pairs.size // 2
    pairs = jnp.asarray(np.concatenate([pairs, np.zeros(2, np.int32)]))

    def head_block(which):
        return pl.BlockSpec((None, s, HEAD_V), lambda hi, bi: (bi, 0, which * n_heads + hi))

    smem = pl.BlockSpec(memory_space=pltpu.SMEM)
    return pl.pallas_call(
        functools.partial(_attn_kernel, tq=tq, tq1=tq1, n_pairs=n_pairs),
        out_shape=jax.ShapeDtypeStruct((b, s, n_heads * HEAD_V), BF16),
        grid=(n_heads, b),
        in_specs=[smem, smem, head_block(0), head_block(1), head_block(2), head_block(3),
                  _const_spec((4, SUB_HEAD)), _const_spec((1, HEAD_V)),
                  _const_spec((n_heads, LANES))],
        out_specs=pl.BlockSpec((None, s, HEAD_V), lambda hi, bi: (bi, 0, hi)),
        scratch_shapes=[pltpu.VMEM((2, s, HEAD_V), BF16),
                        pltpu.VMEM((2, s, HEAD_V), BF16),
                        pltpu.VMEM((2, s, HEAD_V), BF16),
                        pltpu.VMEM((2, s, HEAD_V), BF16),
                        pltpu.VMEM((s, 2 * HEAD_V), BF16),
                        pltpu.VMEM((2, s, 2 * HEAD_V), F32),
                        pltpu.VMEM((2, tq1, HEAD_V), F32),
                        pltpu.VMEM((2, tq1, 1), F32),
                        pltpu.VMEM((2, tq1, 1), F32)],
        compiler_params=_params("parallel", "arbitrary"),
        name="l0_attn",
    )(pairs, score_bound, qkvz, qkvz, qkvz, qkvz, lam_vecs, head_g.reshape(1, HEAD_V), slope_tab)


def _l0_out_l1_in_kernel(o_ref, x_ref, wo_ref, g_ref, wi_ref, x1_ref, u_ref, z_ref):
    e = u_ref.shape[1]
    x1 = x_ref[...] + jnp.dot(o_ref[...], wo_ref[...], preferred_element_type=F32)
    x1_ref[...] = x1
    ms = jnp.mean(x1 * x1, axis=-1, keepdims=True)
    hn = (x1 * lax.rsqrt(ms + EPS) * g_ref[...]).astype(BF16)
    u_ref[...] = jnp.dot(hn, wi_ref[:, :e], preferred_element_type=F32).astype(BF16)
    z_ref[...] = jnp.dot(hn, wi_ref[:, e:], preferred_element_type=F32).astype(BF16)


def _l0_out_l1_in(o, x, wo_bf, norm_g, wi_bf):
    b, s, d = x.shape
    e = o.shape[2]
    tm = _pick(s, 512)
    act_shape = jax.ShapeDtypeStruct((b, s, e), BF16)
    return pl.pallas_call(
        _l0_out_l1_in_kernel,
        out_shape=(jax.ShapeDtypeStruct((b, s, d), F32), act_shape, act_shape),
        grid=(b, s // tm),
        in_specs=[
            pl.BlockSpec((None, tm, e), lambda bi, si: (bi, si, 0)),
            pl.BlockSpec((None, tm, d), lambda bi, si: (bi, si, 0)),
            _const_spec((e, d)),
            _const_spec((1, d)),
            _const_spec((d, 2 * e)),
        ],
        out_specs=(
            pl.BlockSpec((None, tm, d), lambda bi, si: (bi, si, 0)),
            pl.BlockSpec((None, tm, e), lambda bi, si: (bi, si, 0)),
            pl.BlockSpec((None, tm, e), lambda bi, si: (bi, si, 0)),
        ),
        compiler_params=_params("parallel", "parallel"),
        name="l0_out_l1_in",
    )(o, x, wo_bf, norm_g.reshape(1, d), wi_bf)


def _ssm_kernel(u_ref, bt_ref, ct_ref, a_ref, y_ref, il_in_scr, il_out_scr, st0_scr, st1_scr, st2_scr, h_scr):
    batch, n_steps, _ = u_ref.shape
    n_slabs = il_in_scr.shape[0]
    half = batch * n_steps // 2
    c = pl.program_id(1)
    st_scrs = (st0_scr, st1_scr, st2_scr)

    @pl.when(c == 0)
    def _():
        for st in st_scrs:
            st[...] = jnp.zeros(st.shape, F32)
        h_scr[...] = jnp.zeros(h_scr.shape, F32)

    def stages(st_b, st_s, st_c):
        for b in range(batch):
            ub = u_ref[b].astype(F32)
            for k in range(n_slabs):
                il_in_scr[k, pl.ds(b, n_steps, stride=batch), :] = ub[:, k * LANES:(k + 1) * LANES]
        for r in range(2):
            rs = slice(r * half, (r + 1) * half)
            u_il = jnp.concatenate([il_in_scr[k, rs, :] for k in range(n_slabs)], axis=1).astype(BF16)
            st_b[rs, :] = jnp.dot(u_il, bt_ref[...], preferred_element_type=F32)

        for w in range(SSM_TILE_STATE // SCAN_LANES):
            re = pl.ds(w * SCAN_LANES, SCAN_LANES)
            im = pl.ds(SSM_TILE_STATE + w * SCAN_LANES, SCAN_LANES)
            a_re = jnp.broadcast_to(a_ref[0:1, re], (batch, SCAN_LANES))
            a_im = jnp.broadcast_to(a_ref[1:2, re], (batch, SCAN_LANES))
            h_re, h_im = h_scr[:, re], h_scr[:, im]
            for t in range(n_steps):
                rows = pl.ds(t * batch, batch)
                h_re, h_im = (a_re * h_re - a_im * h_im + st_s[rows, re],
                              a_re * h_im + a_im * h_re + st_s[rows, im])
                st_s[rows, re] = h_re
                st_s[rows, im] = h_im
            h_scr[:, re] = h_re
            h_scr[:, im] = h_im

        for r in range(2):
            rs = slice(r * half, (r + 1) * half)
            y = jnp.dot(st_c[rs, :].astype(BF16), ct_ref[...], preferred_element_type=F32)
            for k in range(n_slabs):
                il_out_scr[k, rs, :] = y[:, k * LANES:(k + 1) * LANES]
        for b in range(batch):
            for k in range(n_slabs):
                y_ref[b, :, k * LANES:(k + 1) * LANES] = il_out_scr[
                    k, pl.ds(b, n_steps, stride=batch), :].astype(BF16)

    for res in range(SSM_STAGES):
        pl.when(c % SSM_STAGES == res)(functools.partial(
            stages, st_scrs[res], st_scrs[(res + 2) % SSM_STAGES], st_scrs[(res + 1) % SSM_STAGES]))


def _l1_ssm(u, bt, ct, a_tiles):
    batch, s, e = u.shape
    n_tiles = e // SSM_TILE_CH
    steps = _pick(s, 128)
    rows = steps * batch
    n_chunks = s // steps
    blk_shape = (batch, steps, SSM_TILE_CH)
    st_scratch = pltpu.VMEM((rows, 2 * SSM_TILE_STATE), F32)
    il_scratch = pltpu.VMEM((SSM_TILE_CH // LANES, rows, LANES), F32)
    return pl.pallas_call(
        _ssm_kernel,
        out_shape=jax.ShapeDtypeStruct((batch, s, e), BF16),
        grid=(n_tiles, n_chunks + SSM_STAGES - 1),
        in_specs=[
            pl.BlockSpec(blk_shape, lambda i, c: (0, jnp.minimum(c, n_chunks - 1), i)),
            pl.BlockSpec((None, SSM_TILE_CH, 2 * SSM_TILE_STATE), lambda i, c: (i, 0, 0)),
            pl.BlockSpec((None, 2 * SSM_TILE_STATE, SSM_TILE_CH), lambda i, c: (i, 0, 0)),
            pl.BlockSpec((None, 2, SSM_TILE_STATE), lambda i, c: (i, 0, 0)),
        ],
        out_specs=pl.BlockSpec(blk_shape, lambda i, c: (0, jnp.maximum(c - (SSM_STAGES - 1), 0), i)),
        scratch_shapes=[il_scratch, il_scratch, st_scratch, st_scratch, st_scratch,
                        pltpu.VMEM((batch, 2 * SSM_TILE_STATE), F32)],
        compiler_params=_params("parallel", "arbitrary"),
        name="l1_ssm",
    )(u, bt, ct, a_tiles)


def _glu_out_kernel(y_ref, u_ref, z_ref, x1_ref, d_ref, wg_ref, bg_ref, wo_ref, o_ref):
    y = y_ref[...].astype(F32) + d_ref[...] * u_ref[...].astype(F32)
    cdf = 0.5 * (1.0 + jnp.tanh(math.sqrt(2.0 / math.pi) * (y + 0.044715 * (y * y * y))))
    y = y * cdf
    gate = jnp.dot(y.astype(BF16), wg_ref[...], preferred_element_type=F32) + bg_ref[...]
    y = y * (1.0 / (1.0 + jnp.exp(-gate)))
    z = z_ref[...].astype(F32)
    y = y * (z / (1.0 + jnp.exp(-z)))
    o_ref[...] = x1_ref[...] + jnp.dot(y.astype(BF16), wo_ref[...], preferred_element_type=F32)


def _l1_glu_out(y, u, z, x1, d_skip, wg_bf, b_glu, wo_bf):
    b, s, d = x1.shape
    e = wg_bf.shape[0]
    ts = _pick(s, 256)
    act_spec = pl.BlockSpec((None, ts, e), lambda bi, si: (bi, si, 0))
    return pl.pallas_call(
        _glu_out_kernel,
        out_shape=jax.ShapeDtypeStruct((b, s, d), F32),
        grid=(b, s // ts),
        in_specs=[act_spec, act_spec, act_spec,
                  pl.BlockSpec((None, ts, d), lambda bi, si: (bi, si, 0)),
                  _const_spec((1, e)), _const_spec((e, e)), _const_spec((1, e)),
                  _const_spec((e, d))],
        out_specs=pl.BlockSpec((None, ts, d), lambda bi, si: (bi, si, 0)),
        compiler_params=_params("parallel", "parallel"),
        name="l1_glu_out",
    )(y, u, z, x1, d_skip.reshape(1, e), wg_bf, b_glu.reshape(1, e), wo_bf)


def _s5_tiles(lam_re, lam_im, log_dt, b_re, b_im, c_re, c_im):
    g, p = lam_re.shape
    ch = b_re.shape[2]
    nt = g // SSM_TILE_GROUPS
    dt = jnp.exp(log_dt.astype(F32))[:, None]
    lr, li = lam_re.astype(F32), lam_im.astype(F32)
    mag = jnp.exp(lr * dt)
    ab_re, ab_im = mag * jnp.cos(li * dt), mag * jnp.sin(li * dt)
    den = lr * lr + li * li
    nr, ni = ab_re - 1.0, ab_im
    g_re = (nr * lr + ni * li) / den
    g_im = (ni * lr - nr * li) / den
    br, bi = b_re.astype(F32), b_im.astype(F32)
    bb_re = g_re[..., None] * br - g_im[..., None] * bi
    bb_im = g_re[..., None] * bi + g_im[..., None] * br
    eye = jnp.eye(SSM_TILE_GROUPS, dtype=F32)

    def b_tile(bb):
        t = bb.reshape(nt, SSM_TILE_GROUPS, p, ch)
        t = jnp.einsum("ngpc,gh->ngchp", t, eye)
        return t.reshape(nt, SSM_TILE_GROUPS * ch, SSM_TILE_GROUPS * p)

    def c_tile(cc):
        t = cc.reshape(nt, SSM_TILE_GROUPS, ch, p)
        t = jnp.einsum("ngcp,gh->ngphc", t, eye)
        return t.reshape(nt, SSM_TILE_GROUPS * p, SSM_TILE_GROUPS * ch)

    bt = jnp.concatenate([b_tile(bb_re), b_tile(bb_im)], axis=2).astype(BF16)
    ct = jnp.concatenate([c_tile(c_re.astype(F32)), -c_tile(c_im.astype(F32))], axis=1).astype(BF16)
    a_tiles = jnp.stack([ab_re.reshape(nt, SSM_TILE_STATE), ab_im.reshape(nt, SSM_TILE_STATE)], axis=1)
    return bt, ct, a_tiles


def kernel(x, l0_norm_g, l0_w_in, l0_q_norm_g, l0_k_norm_g, l0_lam_q1, l0_lam_k1, l0_lam_q2, l0_lam_k2, l0_head_norm_g, l0_w_out, l1_norm_g, l1_w_in, l1_lam_re, l1_lam_im, l1_log_dt, l1_b_re, l1_b_im, l1_c_re, l1_c_im, l1_d, l1_w_glu, l1_b_glu, l1_w_out):
    b, s, d = x.shape
    e = l0_w_out.shape[0]
    n_heads = e // HEAD_V
    assert s % LANES == 0 and e % MXU_DIM == 0 and b == SUBLANES

    gq = l0_q_norm_g.astype(F32) * (SUB_HEAD ** -0.5 * LOG2E)
    gk = l0_k_norm_g.astype(F32)
    reps = e // SUB_HEAD
    col_scale = jnp.concatenate([jnp.tile(gq, reps), jnp.tile(gk, reps),
                                 jnp.ones((2 * e,), F32)]).reshape(1, 4 * e)
    qkvz = _l0_inproj(x.reshape(b * s, d), l0_norm_g.astype(F32), l0_w_in.astype(BF16),
                      col_scale, 2 * e).reshape(b, s, 4 * e)

    score_bound = (BF16_ROUNDING_SLACK * SUB_HEAD * jnp.max(jnp.abs(gq)) * jnp.max(jnp.abs(gk))).reshape(1)
    lam_vecs = jnp.stack([l0_lam_q1, l0_lam_k1, l0_lam_q2, l0_lam_k2]).astype(F32)
    o = _l0_attn(qkvz, lam_vecs, l0_head_norm_g.astype(F32), score_bound, n_heads)

    x1, u, z = _l0_out_l1_in(o, x, l0_w_out.astype(BF16), l1_norm_g.astype(F32),
                             l1_w_in.astype(BF16))

    bt, ct, a_tiles = _s5_tiles(l1_lam_re, l1_lam_im, l1_log_dt, l1_b_re, l1_b_im, l1_c_re, l1_c_im)
    y = _l1_ssm(u, bt, ct, a_tiles)

    return _l1_glu_out(y, u, z, x1, l1_d.astype(F32), l1_w_glu.astype(BF16),
                       l1_b_glu.astype(F32), l1_w_out.astype(BF16))
```

```python
import functools
import math

import jax
import jax.numpy as jnp
import numpy as np
from jax import lax
from jax.experimental import pallas as pl
from jax.experimental.pallas import tpu as pltpu

F32 = jnp.float32
BF16 = jnp.bfloat16

EPS = 1e-6
SUB_HEAD = 64
HEAD_V = 2 * SUB_HEAD
GROUP_CH = 16
GROUP_STATE = 64
LAMBDA_INIT_L0 = 0.8 - 0.6 * math.exp(-0.3 * 0)
LOG2E = math.log2(math.e)

LANES = 128
SUBLANES = 8
MXU_DIM = 256
VMEM_LIMIT_BYTES = 56 * 1024 * 1024
NEG_BIG = -1e30
F32_MIN_EXP = 126

INPROJ_TN = 1024
ATTN_TQ = 512
ATTN_TQ_ONLINE = 256
FIXED_SHIFT_MAX_LOG2 = (F32_MIN_EXP - 26) / 2.0
BF16_ROUNDING_SLACK = 1.01
PAIRS_PER_BODY = 7
DIAG_TILES_PER_BODY = 4

SSM_TILE_CH = MXU_DIM
SSM_TILE_GROUPS = SSM_TILE_CH // GROUP_CH
SSM_TILE_STATE = SSM_TILE_GROUPS * GROUP_STATE
SCAN_LANES = 512
SSM_STAGES = 3


def _params(*sem):
    return pltpu.CompilerParams(dimension_semantics=sem, vmem_limit_bytes=VMEM_LIMIT_BYTES)


def _const_spec(shape):
    nd = len(shape)
    return pl.BlockSpec(shape, lambda *_: (0,) * nd, pipeline_mode=pl.Buffered(1))


def _pick(n, pref):
    t = min(n, pref)
    while n % t:
        t //= 2
    return t


def _l0_inproj_kernel(x_ref, g_ref, wqk_ref, wvz_ref, cs_ref, e_ref, qk_ref, vz_ref, h_scr):
    @pl.when(pl.program_id(1) == 0)
    def _():
        xf = x_ref[...]
        ms = jnp.mean(xf * xf, axis=-1, keepdims=True)
        h_scr[...] = (xf * lax.rsqrt(ms + EPS) * g_ref[...]).astype(BF16)

    hn = h_scr[...]
    acc = jnp.dot(hn, wqk_ref[...], preferred_element_type=F32)
    sq = acc * acc
    hi = sq.astype(BF16)
    lo = (sq - hi.astype(F32)).astype(BF16)
    e = e_ref[...]
    for c in range(acc.shape[1] // MXU_DIM):
        sl = slice(c * MXU_DIM, (c + 1) * MXU_DIM)
        ss = (jnp.dot(hi[:, sl], e, preferred_element_type=F32)
              + jnp.dot(lo[:, sl], e, preferred_element_type=F32))
        inv = lax.rsqrt(ss * (1.0 / SUB_HEAD) + EPS)
        qk_ref[:, sl] = (acc[:, sl] * inv * cs_ref[:, sl]).astype(BF16)
    vz_ref[...] = jnp.dot(hn, wvz_ref[...], preferred_element_type=F32).astype(BF16)


def _l0_inproj(x2d, norm_g, w_bf, col_scale):
    m, d = x2d.shape
    n_half = w_bf.shape[1] // 2
    tm = _pick(m, 1024)
    tn = _pick(n_half, INPROJ_TN)
    n_col_tiles = n_half // tn
    seg = np.arange(MXU_DIM) // SUB_HEAD
    e = jnp.asarray(seg[:, None] == seg[None, :], dtype=BF16)
    out = jax.ShapeDtypeStruct((m, n_half), BF16)
    return pl.pallas_call(
        _l0_inproj_kernel,
        out_shape=(out, out),
        grid=(m // tm, n_col_tiles),
        in_specs=[
            pl.BlockSpec((tm, d), lambda i, j: (i, 0)),
            _const_spec((1, d)),
            pl.BlockSpec((d, tn), lambda i, j: (0, j)),
            pl.BlockSpec((d, tn), lambda i, j: (0, j + n_col_tiles)),
            pl.BlockSpec((1, tn), lambda i, j: (0, j)),
            _const_spec((MXU_DIM, MXU_DIM)),
        ],
        out_specs=(pl.BlockSpec((tm, tn), lambda i, j: (i, j)),
                   pl.BlockSpec((tm, tn), lambda i, j: (i, j))),
        scratch_shapes=[pltpu.VMEM((tm, d), BF16)],
        compiler_params=_params("parallel", "arbitrary"),
        name="l0_inproj",
    )(x2d, norm_g.reshape(1, d), w_bf, w_bf, col_scale, e)


def _split3(val):
    p1 = val.astype(BF16).astype(F32)
    r1 = val - p1
    p2 = r1.astype(BF16).astype(F32)
    p3 = (r1 - p2).astype(BF16).astype(F32)
    return p1, p2, p3


def _augment(parts, lane, off, ones_first):
    p_off = off + 3 if ones_first else off
    o_off = off if ones_first else off + 3
    a = jnp.where(lane == p_off, parts[0], 0.0)
    a = jnp.where(lane == p_off + 1, parts[1], a)
    a = jnp.where(lane == p_off + 2, parts[2], a)
    for t in range(3):
        a = jnp.where(lane == o_off + t, 1.0, a)
    return a


def _grouped_loop(n, group, fn):
    def body(t, carry):
        for g in range(group):
            fn(t * group + g)
        return carry

    lax.fori_loop(0, n // group, body, 0)
    for r in range(n - n % group, n):
        fn(r)


def _scores(qa, kk):
    return lax.dot_general(qa, kk, (((1,), (1,)), ((), ())), preferred_element_type=F32)


def _attn_kernel(pairs_ref, shift_ref, q_ref, k_ref, v_ref, z_ref, lam_ref, hg_ref, slope_ref, o_ref,
                 aq_scr, ak_scr, qaug_scr, kaug_scr, vaug_scr, acc_scr, acc1_scr, m_scr, l_scr,
                 *, tq, tq1, n_pairs):
    s_len = q_ref.shape[0]
    n_blk = s_len // tq
    half = tq // 2
    h = pl.program_id(0)
    shift = shift_ref[0]
    slope = slope_ref[pl.ds(h, 1), :] * LOG2E
    lane = lax.broadcasted_iota(jnp.int32, (tq, LANES), 1)
    row = lax.broadcasted_iota(jnp.int32, (tq, LANES), 0)

    lam_v = lam_ref[...]
    lam = (jnp.exp(jnp.sum(lam_v[0:1] * lam_v[1:2], axis=1, keepdims=True))
           - jnp.exp(jnp.sum(lam_v[2:3] * lam_v[3:4], axis=1, keepdims=True)) + LAMBDA_INIT_L0)

    def blk(i, size):
        return pl.ds(pl.multiple_of(i * size, size), size)

    def finish(rows, o0, l0, o1, l1):
        o = o0 * (1.0 / l0) - lam * (o1 * (1.0 / l1))
        ms = jnp.mean(o * o, axis=-1, keepdims=True)
        on = o * lax.rsqrt(ms + EPS) * hg_ref[...] * (1.0 - LAMBDA_INIT_L0)
        z = z_ref[rows, :].astype(F32)
        o_ref[rows, :] = (on * (z / (1.0 + jnp.exp(-z)))).astype(BF16)

    @pl.when(pl.program_id(1) == 0)
    def _tables():
        ones_col = jnp.ones((tq, HEAD_V), BF16)

        def body(c, carry):
            rows = blk(c, tq)
            pos = slope * (c * tq + row).astype(F32)
            k_parts = _split3(pos)
            q_parts = _split3(-(pos + shift))
            ak_scr[0, rows, :] = _augment(k_parts, lane, SUB_HEAD, True).astype(BF16)
            ak_scr[1, rows, :] = _augment(k_parts, lane, 0, True).astype(BF16)
            aq_scr[0, rows, :] = _augment(q_parts, lane, SUB_HEAD, False).astype(BF16)
            aq_scr[1, rows, :] = _augment(q_parts, lane, 0, False).astype(BF16)
            vaug_scr[rows, HEAD_V:2 * HEAD_V] = ones_col
            return carry

        lax.fori_loop(0, n_blk, body, 0)

    lane_row = lax.broadcasted_iota(jnp.int32, (1, LANES), 1)
    keep = (jnp.where(lane_row < SUB_HEAD, 1.0, 0.0).astype(BF16),
            jnp.where(lane_row < SUB_HEAD, 0.0, 1.0).astype(BF16))

    def build(c, carry):
        rows = blk(c, tq)
        kb = k_ref[rows, :]
        qb = q_ref[rows, :]
        for sub in range(2):
            kaug_scr[sub, rows, :] = kb * keep[sub] + ak_scr[sub, rows, :]
            qaug_scr[sub, rows, :] = qb * keep[sub] + aq_scr[sub, rows, :]
        vaug_scr[rows, 0:HEAD_V] = v_ref[rows, :]
        return carry

    lax.fori_loop(0, n_blk, build, 0)

    fixed_shift_ok = shift <= FIXED_SHIFT_MAX_LOG2

    @pl.when(fixed_shift_ok)
    def _fixed_shift():
        def tile(q_rows, k_rows, mask_off, init):
            n_q, n_k = q_rows.size, k_rows.size
            for sub in range(2):
                s = _scores(qaug_scr[sub, q_rows, :], kaug_scr[sub, k_rows, :])
                if mask_off is not None:
                    cols = lax.broadcasted_iota(jnp.int32, (n_q, n_k), 1)
                    rws = lax.broadcasted_iota(jnp.int32, (n_q, n_k), 0)
                    s = jnp.where(cols <= rws + mask_off, s, -jnp.inf)
                pv = jnp.dot(jnp.exp2(s).astype(BF16), vaug_scr[k_rows, :], preferred_element_type=F32)
                if init:
                    acc_scr[sub, q_rows, :] = pv
                else:
                    acc_scr[sub, q_rows, :] += pv

        def diagonal(i):
            r0 = pl.multiple_of(i * tq, tq)
            tile(pl.ds(r0, half), pl.ds(r0, half), 0, True)
            tile(pl.ds(r0 + half, half), pl.ds(r0, tq), half, True)

        def off_diagonal(p):
            tile(blk(pairs_ref[2 * p], tq), blk(pairs_ref[2 * p + 1], tq), None, False)

        _grouped_loop(n_blk, DIAG_TILES_PER_BODY, diagonal)
        _grouped_loop(n_pairs, PAIRS_PER_BODY, off_diagonal)

        def finalize(i, carry):
            rows = blk(i, tq)
            a0 = acc_scr[0, rows, :]
            a1 = acc_scr[1, rows, :]
            finish(rows, a0[:, :HEAD_V], a0[:, HEAD_V:], a1[:, :HEAD_V], a1[:, HEAD_V:])
            return carry

        lax.fori_loop(0, n_blk, finalize, 0)

    @pl.when(jnp.logical_not(fixed_shift_ok))
    def _online():
        col_s = lax.broadcasted_iota(jnp.int32, (tq1, tq1), 1)
        row_s = lax.broadcasted_iota(jnp.int32, (tq1, tq1), 0)
        causal = col_s <= row_s

        def q_block(i, carry):
            rows = blk(i, tq1)
            m_scr[...] = jnp.full(m_scr.shape, NEG_BIG, F32)
            l_scr[...] = jnp.zeros(l_scr.shape, F32)
            acc1_scr[...] = jnp.zeros(acc1_scr.shape, F32)

            def kv_step(j, masked):
                k_rows = blk(j, tq1)
                v = v_ref[k_rows, :]
                for sub in range(2):
                    s = _scores(qaug_scr[sub, rows, :], kaug_scr[sub, k_rows, :])
                    if masked:
                        s = jnp.where(causal, s, -jnp.inf)
                    m_prev = m_scr[sub]
                    m_new = jnp.maximum(m_prev, jnp.max(s, axis=1, keepdims=True))
                    alpha = jnp.exp2(m_prev - m_new)
                    p = jnp.exp2(s - m_new)
                    l_scr[sub] = alpha * l_scr[sub] + jnp.sum(p, axis=1, keepdims=True)
                    acc1_scr[sub] = alpha * acc1_scr[sub] + jnp.dot(
                        p.astype(BF16), v, preferred_element_type=F32)
                    m_scr[sub] = m_new

            def off_diag(j, c):
                kv_step(j, False)
                return c

            lax.fori_loop(0, i, off_diag, 0)
            kv_step(i, True)
            finish(rows, acc1_scr[0], l_scr[0], acc1_scr[1], l_scr[1])
            return carry

        lax.fori_loop(0, s_len // tq1, q_block, 0)


def _l0_attn(qk, vz, lam_vecs, head_g, score_bound, n_heads):
    b, s, _ = qk.shape
    tq = _pick(s, ATTN_TQ)
    tq1 = _pick(s, ATTN_TQ_ONLINE)
    n_blk = s // tq
    slopes = 2.0 ** (-8.0 * np.arange(1, n_heads + 1) / n_heads)
    slope_tab = jnp.asarray(np.broadcast_to(slopes[:, None], (n_heads, LANES)), dtype=F32)
    pairs = np.asarray([(i, j) for j in range(n_blk) for i in range(j + 1, n_blk)],
                       dtype=np.int32).reshape(-1)
    n_pairs = pairs.size // 2
    pairs = jnp.asarray(np.concatenate([pairs, np.zeros(2, np.int32)]))

    def head_block(which):
        return pl.BlockSpec((None, s, HEAD_V), lambda hi, bi: (bi, 0, which * n_heads + hi))

    smem = pl.BlockSpec(memory_space=pltpu.SMEM)
    return pl.pallas_call(
        functools.partial(_attn_kernel, tq=tq, tq1=tq1, n_pairs=n_pairs),
        out_shape=jax.ShapeDtypeStruct((b, s, n_heads * HEAD_V), BF16),
        grid=(n_heads, b),
        in_specs=[smem, smem, head_block(0), head_block(1), head_block(0), head_block(1),
                  _const_spec((4, SUB_HEAD)), _const_spec((1, HEAD_V)),
                  _const_spec((n_heads, LANES))],
        out_specs=pl.BlockSpec((None, s, HEAD_V), lambda hi, bi: (bi, 0, hi)),
        scratch_shapes=[pltpu.VMEM((2, s, HEAD_V), BF16),
                        pltpu.VMEM((2, s, HEAD_V), BF16),
                        pltpu.VMEM((2, s, HEAD_V), BF16),
                        pltpu.VMEM((2, s, HEAD_V), BF16),
                        pltpu.VMEM((s, 2 * HEAD_V), BF16),
                        pltpu.VMEM((2, s, 2 * HEAD_V), F32),
                        pltpu.VMEM((2, tq1, HEAD_V), F32),
                        pltpu.VMEM((2, tq1, 1), F32),
                        pltpu.VMEM((2, tq1, 1), F32)],
        compiler_params=_params("parallel", "arbitrary"),
        name="l0_attn",
    )(pairs, score_bound, qk, qk, vz, vz, lam_vecs, head_g.reshape(1, HEAD_V), slope_tab)


def _l0_out_l1_in_kernel(o_ref, x_ref, wo_ref, g_ref, wi_ref, x1_ref, u_ref, z_ref):
    e = u_ref.shape[1]
    x1 = x_ref[...] + jnp.dot(o_ref[...], wo_ref[...], preferred_element_type=F32)
    x1_ref[...] = x1
    ms = jnp.mean(x1 * x1, axis=-1, keepdims=True)
    hn = (x1 * lax.rsqrt(ms + EPS) * g_ref[...]).astype(BF16)
    u_ref[...] = jnp.dot(hn, wi_ref[:, :e], preferred_element_type=F32).astype(BF16)
    z_ref[...] = jnp.dot(hn, wi_ref[:, e:], preferred_element_type=F32).astype(BF16)


def _l0_out_l1_in(o, x, wo_bf, norm_g, wi_bf):
    b, s, d = x.shape
    e = o.shape[2]
    tm = _pick(s, 512)
    act_shape = jax.ShapeDtypeStruct((b, s, e), BF16)
    return pl.pallas_call(
        _l0_out_l1_in_kernel,
        out_shape=(jax.ShapeDtypeStruct((b, s, d), F32), act_shape, act_shape),
        grid=(b, s // tm),
        in_specs=[
            pl.BlockSpec((None, tm, e), lambda bi, si: (bi, si, 0)),
            pl.BlockSpec((None, tm, d), lambda bi, si: (bi, si, 0)),
            _const_spec((e, d)),
            _const_spec((1, d)),
            _const_spec((d, 2 * e)),
        ],
        out_specs=(
            pl.BlockSpec((None, tm, d), lambda bi, si: (bi, si, 0)),
            pl.BlockSpec((None, tm, e), lambda bi, si: (bi, si, 0)),
            pl.BlockSpec((None, tm, e), lambda bi, si: (bi, si, 0)),
        ),
        compiler_params=_params("parallel", "parallel"),
        name="l0_out_l1_in",
    )(o, x, wo_bf, norm_g.reshape(1, d), wi_bf)


def _ssm_kernel(u_ref, bt_ref, ct_ref, a_ref, y_ref, il_in_scr, il_out_scr, st0_scr, st1_scr, st2_scr, h_scr):
    batch, n_steps, _ = u_ref.shape
    n_slabs = il_in_scr.shape[0]
    half = batch * n_steps // 2
    c = pl.program_id(1)
    st_scrs = (st0_scr, st1_scr, st2_scr)

    @pl.when(c == 0)
    def _():
        for st in st_scrs:
            st[...] = jnp.zeros(st.shape, F32)
        h_scr[...] = jnp.zeros(h_scr.shape, F32)

    def stages(st_b, st_s, st_c):
        for b in range(batch):
            ub = u_ref[b].astype(F32)
            for k in range(n_slabs):
                il_in_scr[k, pl.ds(b, n_steps, stride=batch), :] = ub[:, k * LANES:(k + 1) * LANES]
        for r in range(2):
            rs = slice(r * half, (r + 1) * half)
            u_il = jnp.concatenate([il_in_scr[k, rs, :] for k in range(n_slabs)], axis=1).astype(BF16)
            st_b[rs, :] = jnp.dot(u_il, bt_ref[...], preferred_element_type=F32)

        for w in range(SSM_TILE_STATE // SCAN_LANES):
            re = pl.ds(w * SCAN_LANES, SCAN_LANES)
            im = pl.ds(SSM_TILE_STATE + w * SCAN_LANES, SCAN_LANES)
            a_re = jnp.broadcast_to(a_ref[0:1, re], (batch, SCAN_LANES))
            a_im = jnp.broadcast_to(a_ref[1:2, re], (batch, SCAN_LANES))
            h_re, h_im = h_scr[:, re], h_scr[:, im]
            for t in range(n_steps):
                rows = pl.ds(t * batch, batch)
                h_re, h_im = (a_re * h_re - a_im * h_im + st_s[rows, re],
                              a_re * h_im + a_im * h_re + st_s[rows, im])
                st_s[rows, re] = h_re
                st_s[rows, im] = h_im
            h_scr[:, re] = h_re
            h_scr[:, im] = h_im

        for r in range(2):
            rs = slice(r * half, (r + 1) * half)
            y = jnp.dot(st_c[rs, :].astype(BF16), ct_ref[...], preferred_element_type=F32)
            for k in range(n_slabs):
                il_out_scr[k, rs, :] = y[:, k * LANES:(k + 1) * LANES]
        for b in range(batch):
            for k in range(n_slabs):
                y_ref[b, :, k * LANES:(k + 1) * LANES] = il_out_scr[
                    k, pl.ds(b, n_steps, stride=batch), :].astype(BF16)

    for res in range(SSM_STAGES):
        pl.when(c % SSM_STAGES == res)(functools.partial(
            stages, st_scrs[res], st_scrs[(res + 2) % SSM_STAGES], st_scrs[(res + 1) % SSM_STAGES]))


def _l1_ssm(u, bt, ct, a_tiles):
    batch, s, e = u.shape
    n_tiles = e // SSM_TILE_CH
    steps = _pick(s, 128)
    rows = steps * batch
    n_chunks = s // steps
    blk_shape = (batch, steps, SSM_TILE_CH)
    st_scratch = pltpu.VMEM((rows, 2 * SSM_TILE_STATE), F32)
    il_scratch = pltpu.VMEM((SSM_TILE_CH // LANES, rows, LANES), F32)
    return pl.pallas_call(
        _ssm_kernel,
        out_shape=jax.ShapeDtypeStruct((batch, s, e), BF16),
        grid=(n_tiles, n_chunks + SSM_STAGES - 1),
        in_specs=[
            pl.BlockSpec(blk_shape, lambda i, c: (0, jnp.minimum(c, n_chunks - 1), i)),
            pl.BlockSpec((None, SSM_TILE_CH, 2 * SSM_TILE_STATE), lambda i, c: (i, 0, 0)),
            pl.BlockSpec((None, 2 * SSM_TILE_STATE, SSM_TILE_CH), lambda i, c: (i, 0, 0)),
            pl.BlockSpec((None, 2, SSM_TILE_STATE), lambda i, c: (i, 0, 0)),
        ],
        out_specs=pl.BlockSpec(blk_shape, lambda i, c: (0, jnp.maximum(c - (SSM_STAGES - 1), 0), i)),
        scratch_shapes=[il_scratch, il_scratch, st_scratch, st_scratch, st_scratch,
                        pltpu.VMEM((batch, 2 * SSM_TILE_STATE), F32)],
        compiler_params=_params("parallel", "arbitrary"),
        name="l1_ssm",
    )(u, bt, ct, a_tiles)


def _glu_out_kernel(y_ref, u_ref, z_ref, x1_ref, d_ref, wg_ref, bg_ref, wo_ref, o_ref):
    y = y_ref[...].astype(F32) + d_ref[...] * u_ref[...].astype(F32)
    cdf = 0.5 * (1.0 + jnp.tanh(math.sqrt(2.0 / math.pi) * (y + 0.044715 * (y * y * y))))
    y = y * cdf
    gate = jnp.dot(y.astype(BF16), wg_ref[...], preferred_element_type=F32) + bg_ref[...]
    y = y * (1.0 / (1.0 + jnp.exp(-gate)))
    z = z_ref[...].astype(F32)
    y = y * (z / (1.0 + jnp.exp(-z)))
    o_ref[...] = x1_ref[...] + jnp.dot(y.astype(BF16), wo_ref[...], preferred_element_type=F32)


def _l1_glu_out(y, u, z, x1, d_skip, wg_bf, b_glu, wo_bf):
    b, s, d = x1.shape
    e = wg_bf.shape[0]
    ts = _pick(s, 256)
    act_spec = pl.BlockSpec((None, ts, e), lambda bi, si: (bi, si, 0))
    return pl.pallas_call(
        _glu_out_kernel,
        out_shape=jax.ShapeDtypeStruct((b, s, d), F32),
        grid=(b, s // ts),
        in_specs=[act_spec, act_spec, act_spec,
                  pl.BlockSpec((None, ts, d), lambda bi, si: (bi, si, 0)),
                  _const_spec((1, e)), _const_spec((e, e)), _const_spec((1, e)),
                  _const_spec((e, d))],
        out_specs=pl.BlockSpec((None, ts, d), lambda bi, si: (bi, si, 0)),
        compiler_params=_params("parallel", "parallel"),
        name="l1_glu_out",
    )(y, u, z, x1, d_skip.reshape(1, e), wg_bf, b_glu.reshape(1, e), wo_bf)


def _s5_tiles(lam_re, lam_im, log_dt, b_re, b_im, c_re, c_im):
    g, p = lam_re.shape
    ch = b_re.shape[2]
    nt = g // SSM_TILE_GROUPS
    dt = jnp.exp(log_dt.astype(F32))[:, None]
    lr, li = lam_re.astype(F32), lam_im.astype(F32)
    mag = jnp.exp(lr * dt)
    ab_re, ab_im = mag * jnp.cos(li * dt), mag * jnp.sin(li * dt)
    den = lr * lr + li * li
    nr, ni = ab_re - 1.0, ab_im
    g_re = (nr * lr + ni * li) / den
    g_im = (ni * lr - nr * li) / den
    br, bi = b_re.astype(F32), b_im.astype(F32)
    bb_re = g_re[..., None] * br - g_im[..., None] * bi
    bb_im = g_re[..., None] * bi + g_im[..., None] * br
    eye = jnp.eye(SSM_TILE_GROUPS, dtype=F32)

    def b_tile(bb):
        t = bb.reshape(nt, SSM_TILE_GROUPS, p, ch)
        t = jnp.einsum("ngpc,gh->ngchp", t, eye)
        return t.reshape(nt, SSM_TILE_GROUPS * ch, SSM_TILE_GROUPS * p)

    def c_tile(cc):
        t = cc.reshape(nt, SSM_TILE_GROUPS, ch, p)
        t = jnp.einsum("ngcp,gh->ngphc", t, eye)
        return t.reshape(nt, SSM_TILE_GROUPS * p, SSM_TILE_GROUPS * ch)

    bt = jnp.concatenate([b_tile(bb_re), b_tile(bb_im)], axis=2).astype(BF16)
    ct = jnp.concatenate([c_tile(c_re.astype(F32)), -c_tile(c_im.astype(F32))], axis=1).astype(BF16)
    a_tiles = jnp.stack([ab_re.reshape(nt, SSM_TILE_STATE), ab_im.reshape(nt, SSM_TILE_STATE)], axis=1)
    return bt, ct, a_tiles


def kernel(x, l0_norm_g, l0_w_in, l0_q_norm_g, l0_k_norm_g, l0_lam_q1, l0_lam_k1, l0_lam_q2, l0_lam_k2, l0_head_norm_g, l0_w_out, l1_norm_g, l1_w_in, l1_lam_re, l1_lam_im, l1_log_dt, l1_b_re, l1_b_im, l1_c_re, l1_c_im, l1_d, l1_w_glu, l1_b_glu, l1_w_out):
    b, s, d = x.shape
    e = l0_w_out.shape[0]
    n_heads = e // HEAD_V
    assert s % LANES == 0 and e % MXU_DIM == 0 and b == SUBLANES

    gq = l0_q_norm_g.astype(F32) * (SUB_HEAD ** -0.5 * LOG2E)
    gk = l0_k_norm_g.astype(F32)
    reps = e // SUB_HEAD
    col_scale = jnp.concatenate([jnp.tile(gq, reps), jnp.tile(gk, reps)]).reshape(1, 2 * e)
    qk, vz = _l0_inproj(x.reshape(b * s, d), l0_norm_g.astype(F32), l0_w_in.astype(BF16), col_scale)

    score_bound = (BF16_ROUNDING_SLACK * SUB_HEAD * jnp.max(jnp.abs(gq)) * jnp.max(jnp.abs(gk))).reshape(1)
    lam_vecs = jnp.stack([l0_lam_q1, l0_lam_k1, l0_lam_q2, l0_lam_k2]).astype(F32)
    o = _l0_attn(qk.reshape(b, s, 2 * e), vz.reshape(b, s, 2 * e), lam_vecs,
                 l0_head_norm_g.astype(F32), score_bound, n_heads)

    x1, u, z = _l0_out_l1_in(o, x, l0_w_out.astype(BF16), l1_norm_g.astype(F32),
                             l1_w_in.astype(BF16))

    bt, ct, a_tiles = _s5_tiles(l1_lam_re, l1_lam_im, l1_log_dt, l1_b_re, l1_b_im, l1_c_re, l1_c_im)
    y = _l1_ssm(u, bt, ct, a_tiles)

    return _l1_glu_out(y, u, z, x1, l1_d.astype(F32), l1_w_glu.astype(BF16),
                       l1_b_glu.astype(F32), l1_w_out.astype(BF16))
```

```python
import functools
import math

import jax
import jax.numpy as jnp
import numpy as np
from jax import lax
from jax.experimental import pallas as pl
from jax.experimental.pallas import tpu as pltpu

F32 = jnp.float32
BF16 = jnp.bfloat16

EPS = 1e-6
SUB_HEAD = 64
HEAD_V = 2 * SUB_HEAD
GROUP_CH = 16
GROUP_STATE = 64
LAMBDA_INIT_L0 = 0.8 - 0.6 * math.exp(-0.3 * 0)
LOG2E = math.log2(math.e)

LANES = 128
SUBLANES = 8
MXU_DIM = 256
VMEM_LIMIT_BYTES = 56 * 1024 * 1024
NEG_BIG = -1e30
F32_MIN_EXP = 126

INPROJ_TN = 1024
ATTN_TQ = 512
ATTN_TQ_ONLINE = 256
FIXED_SHIFT_MAX_LOG2 = (F32_MIN_EXP - 26) / 2.0
BF16_ROUNDING_SLACK = 1.01
PAIRS_PER_BODY = 7
DIAG_TILES_PER_BODY = 4

SSM_TILE_CH = MXU_DIM
SSM_TILE_GROUPS = SSM_TILE_CH // GROUP_CH
SSM_TILE_STATE = SSM_TILE_GROUPS * GROUP_STATE
SCAN_LANES = 256
GLU_ROWS = 512
GLU_ROW_SPLITS = 2
SSM_STAGES = 3


def _params(*sem):
    return pltpu.CompilerParams(dimension_semantics=sem, vmem_limit_bytes=VMEM_LIMIT_BYTES)


def _const_spec(shape):
    nd = len(shape)
    return pl.BlockSpec(shape, lambda *_: (0,) * nd, pipeline_mode=pl.Buffered(1))


def _pick(n, pref):
    t = min(n, pref)
    while n % t:
        t //= 2
    return t


def _l0_inproj_kernel(x_ref, g_ref, wqk_ref, wvz_ref, cs_ref, e_ref, qk_ref, vz_ref, h_scr):
    @pl.when(pl.program_id(1) == 0)
    def _():
        xf = x_ref[...]
        ms = jnp.mean(xf * xf, axis=-1, keepdims=True)
        h_scr[...] = (xf * lax.rsqrt(ms + EPS) * g_ref[...]).astype(BF16)

    hn = h_scr[...]
    acc = jnp.dot(hn, wqk_ref[...], preferred_element_type=F32)
    sq = acc * acc
    hi = sq.astype(BF16)
    lo = (sq - hi.astype(F32)).astype(BF16)
    e = e_ref[...]
    for c in range(acc.shape[1] // MXU_DIM):
        sl = slice(c * MXU_DIM, (c + 1) * MXU_DIM)
        ss = (jnp.dot(hi[:, sl], e, preferred_element_type=F32)
              + jnp.dot(lo[:, sl], e, preferred_element_type=F32))
        inv = lax.rsqrt(ss * (1.0 / SUB_HEAD) + EPS)
        qk_ref[:, sl] = (acc[:, sl] * inv * cs_ref[:, sl]).astype(BF16)
    vz_ref[...] = jnp.dot(hn, wvz_ref[...], preferred_element_type=F32).astype(BF16)


def _l0_inproj(x2d, norm_g, w_bf, col_scale):
    m, d = x2d.shape
    n_half = w_bf.shape[1] // 2
    tm = _pick(m, 1024)
    tn = _pick(n_half, INPROJ_TN)
    n_col_tiles = n_half // tn
    seg = np.arange(MXU_DIM) // SUB_HEAD
    e = jnp.asarray(seg[:, None] == seg[None, :], dtype=BF16)
    out = jax.ShapeDtypeStruct((m, n_half), BF16)
    return pl.pallas_call(
        _l0_inproj_kernel,
        out_shape=(out, out),
        grid=(m // tm, n_col_tiles),
        in_specs=[
            pl.BlockSpec((tm, d), lambda i, j: (i, 0)),
            _const_spec((1, d)),
            pl.BlockSpec((d, tn), lambda i, j: (0, j)),
            pl.BlockSpec((d, tn), lambda i, j: (0, j + n_col_tiles)),
            pl.BlockSpec((1, tn), lambda i, j: (0, j)),
            _const_spec((MXU_DIM, MXU_DIM)),
        ],
        out_specs=(pl.BlockSpec((tm, tn), lambda i, j: (i, j)),
                   pl.BlockSpec((tm, tn), lambda i, j: (i, j))),
        scratch_shapes=[pltpu.VMEM((tm, d), BF16)],
        compiler_params=_params("parallel", "arbitrary"),
        name="l0_inproj",
    )(x2d, norm_g.reshape(1, d), w_bf, w_bf, col_scale, e)


def _split3(val):
    p1 = val.astype(BF16).astype(F32)
    r1 = val - p1
    p2 = r1.astype(BF16).astype(F32)
    p3 = (r1 - p2).astype(BF16).astype(F32)
    return p1, p2, p3


def _augment(parts, lane, off, ones_first):
    p_off = off + 3 if ones_first else off
    o_off = off if ones_first else off + 3
    a = jnp.where(lane == p_off, parts[0], 0.0)
    a = jnp.where(lane == p_off + 1, parts[1], a)
    a = jnp.where(lane == p_off + 2, parts[2], a)
    for t in range(3):
        a = jnp.where(lane == o_off + t, 1.0, a)
    return a


def _grouped_loop(n, group, fn):
    def body(t, carry):
        for g in range(group):
            fn(t * group + g)
        return carry

    lax.fori_loop(0, n // group, body, 0)
    for r in range(n - n % group, n):
        fn(r)


def _scores(qa, kk):
    return lax.dot_general(qa, kk, (((1,), (1,)), ((), ())), preferred_element_type=F32)


def _attn_kernel(pairs_ref, shift_ref, q_ref, k_ref, v_ref, z_ref, lam_ref, hg_ref, slope_ref, o_ref,
                 aq_scr, ak_scr, qaug_scr, kaug_scr, vaug_scr, acc_scr, acc1_scr, m_scr, l_scr,
                 *, tq, tq1, n_pairs):
    s_len = q_ref.shape[0]
    n_blk = s_len // tq
    half = tq // 2
    h = pl.program_id(0)
    shift = shift_ref[0]
    slope = slope_ref[pl.ds(h, 1), :] * LOG2E
    lane = lax.broadcasted_iota(jnp.int32, (tq, LANES), 1)
    row = lax.broadcasted_iota(jnp.int32, (tq, LANES), 0)

    lam_v = lam_ref[...]
    lam = (jnp.exp(jnp.sum(lam_v[0:1] * lam_v[1:2], axis=1, keepdims=True))
           - jnp.exp(jnp.sum(lam_v[2:3] * lam_v[3:4], axis=1, keepdims=True)) + LAMBDA_INIT_L0)

    def blk(i, size):
        return pl.ds(pl.multiple_of(i * size, size), size)

    def finish(rows, o0, l0, o1, l1):
        o = o0 * (1.0 / l0) - lam * (o1 * (1.0 / l1))
        ms = jnp.mean(o * o, axis=-1, keepdims=True)
        on = o * lax.rsqrt(ms + EPS) * hg_ref[...] * (1.0 - LAMBDA_INIT_L0)
        z = z_ref[rows, :].astype(F32)
        o_ref[rows, :] = (on * (z / (1.0 + jnp.exp(-z)))).astype(BF16)

    @pl.when(pl.program_id(1) == 0)
    def _tables():
        ones_col = jnp.ones((tq, HEAD_V), BF16)

        def body(c, carry):
            rows = blk(c, tq)
            pos = slope * (c * tq + row).astype(F32)
            k_parts = _split3(pos)
            q_parts = _split3(-(pos + shift))
            ak_scr[0, rows, :] = _augment(k_parts, lane, SUB_HEAD, True).astype(BF16)
            ak_scr[1, rows, :] = _augment(k_parts, lane, 0, True).astype(BF16)
            aq_scr[0, rows, :] = _augment(q_parts, lane, SUB_HEAD, False).astype(BF16)
            aq_scr[1, rows, :] = _augment(q_parts, lane, 0, False).astype(BF16)
            vaug_scr[rows, HEAD_V:2 * HEAD_V] = ones_col
            return carry

        lax.fori_loop(0, n_blk, body, 0)

    lane_row = lax.broadcasted_iota(jnp.int32, (1, LANES), 1)
    keep = (jnp.where(lane_row < SUB_HEAD, 1.0, 0.0).astype(BF16),
            jnp.where(lane_row < SUB_HEAD, 0.0, 1.0).astype(BF16))

    def build(c, carry):
        rows = blk(c, tq)
        kb = k_ref[rows, :]
        qb = q_ref[rows, :]
        for sub in range(2):
            kaug_scr[sub, rows, :] = kb * keep[sub] + ak_scr[sub, rows, :]
            qaug_scr[sub, rows, :] = qb * keep[sub] + aq_scr[sub, rows, :]
        vaug_scr[rows, 0:HEAD_V] = v_ref[rows, :]
        return carry

    lax.fori_loop(0, n_blk, build, 0)

    fixed_shift_ok = shift <= FIXED_SHIFT_MAX_LOG2

    @pl.when(fixed_shift_ok)
    def _fixed_shift():
        def tile(q_rows, k_rows, mask_off, init):
            n_q, n_k = q_rows.size, k_rows.size
            for sub in range(2):
                s = _scores(qaug_scr[sub, q_rows, :], kaug_scr[sub, k_rows, :])
                if mask_off is not None:
                    cols = lax.broadcasted_iota(jnp.int32, (n_q, n_k), 1)
                    rws = lax.broadcasted_iota(jnp.int32, (n_q, n_k), 0)
                    s = jnp.where(cols <= rws + mask_off, s, -jnp.inf)
                pv = jnp.dot(jnp.exp2(s).astype(BF16), vaug_scr[k_rows, :], preferred_element_type=F32)
                if init:
                    acc_scr[sub, q_rows, :] = pv
                else:
                    acc_scr[sub, q_rows, :] += pv

        def diagonal(i):
            r0 = pl.multiple_of(i * tq, tq)
            tile(pl.ds(r0, half), pl.ds(r0, half), 0, True)
            tile(pl.ds(r0 + half, half), pl.ds(r0, tq), half, True)

        def off_diagonal(p):
            tile(blk(pairs_ref[2 * p], tq), blk(pairs_ref[2 * p + 1], tq), None, False)

        _grouped_loop(n_blk, DIAG_TILES_PER_BODY, diagonal)
        _grouped_loop(n_pairs, PAIRS_PER_BODY, off_diagonal)

        def finalize(i, carry):
            rows = blk(i, tq)
            a0 = acc_scr[0, rows, :]
            a1 = acc_scr[1, rows, :]
            finish(rows, a0[:, :HEAD_V], a0[:, HEAD_V:], a1[:, :HEAD_V], a1[:, HEAD_V:])
            return carry

        lax.fori_loop(0, n_blk, finalize, 0)

    @pl.when(jnp.logical_not(fixed_shift_ok))
    def _online():
        col_s = lax.broadcasted_iota(jnp.int32, (tq1, tq1), 1)
        row_s = lax.broadcasted_iota(jnp.int32, (tq1, tq1), 0)
        causal = col_s <= row_s

        def q_block(i, carry):
            rows = blk(i, tq1)
            m_scr[...] = jnp.full(m_scr.shape, NEG_BIG, F32)
            l_scr[...] = jnp.zeros(l_scr.shape, F32)
            acc1_scr[...] = jnp.zeros(acc1_scr.shape, F32)

            def kv_step(j, masked):
                k_rows = blk(j, tq1)
                v = v_ref[k_rows, :]
                for sub in range(2):
                    s = _scores(qaug_scr[sub, rows, :], kaug_scr[sub, k_rows, :])
                    if masked:
                        s = jnp.where(causal, s, -jnp.inf)
                    m_prev = m_scr[sub]
                    m_new = jnp.maximum(m_prev, jnp.max(s, axis=1, keepdims=True))
                    alpha = jnp.exp2(m_prev - m_new)
                    p = jnp.exp2(s - m_new)
                    l_scr[sub] = alpha * l_scr[sub] + jnp.sum(p, axis=1, keepdims=True)
                    acc1_scr[sub] = alpha * acc1_scr[sub] + jnp.dot(
                        p.astype(BF16), v, preferred_element_type=F32)
                    m_scr[sub] = m_new

            def off_diag(j, c):
                kv_step(j, False)
                return c

            lax.fori_loop(0, i, off_diag, 0)
            kv_step(i, True)
            finish(rows, acc1_scr[0], l_scr[0], acc1_scr[1], l_scr[1])
            return carry

        lax.fori_loop(0, s_len // tq1, q_block, 0)


def _l0_attn(qk, vz, lam_vecs, head_g, score_bound, n_heads):
    b, s, _ = qk.shape
    tq = _pick(s, ATTN_TQ)
    tq1 = _pick(s, ATTN_TQ_ONLINE)
    n_blk = s // tq
    slopes = 2.0 ** (-8.0 * np.arange(1, n_heads + 1) / n_heads)
    slope_tab = jnp.asarray(np.broadcast_to(slopes[:, None], (n_heads, LANES)), dtype=F32)
    pairs = np.asarray([(i, j) for j in range(n_blk) for i in range(j + 1, n_blk)],
                       dtype=np.int32).reshape(-1)
    n_pairs = pairs.size // 2
    pairs = jnp.asarray(np.concatenate([pairs, np.zeros(2, np.int32)]))

    def head_block(which):
        return pl.BlockSpec((None, s, HEAD_V), lambda hi, bi: (bi, 0, which * n_heads + hi))

    smem = pl.BlockSpec(memory_space=pltpu.SMEM)
    return pl.pallas_call(
        functools.partial(_attn_kernel, tq=tq, tq1=tq1, n_pairs=n_pairs),
        out_shape=jax.ShapeDtypeStruct((b, s, n_heads * HEAD_V), BF16),
        grid=(n_heads, b),
        in_specs=[smem, smem, head_block(0), head_block(1), head_block(0), head_block(1),
                  _const_spec((4, SUB_HEAD)), _const_spec((1, HEAD_V)),
                  _const_spec((n_heads, LANES))],
        out_specs=pl.BlockSpec((None, s, HEAD_V), lambda hi, bi: (bi, 0, hi)),
        scratch_shapes=[pltpu.VMEM((2, s, HEAD_V), BF16),
                        pltpu.VMEM((2, s, HEAD_V), BF16),
                        pltpu.VMEM((2, s, HEAD_V), BF16),
                        pltpu.VMEM((2, s, HEAD_V), BF16),
                        pltpu.VMEM((s, 2 * HEAD_V), BF16),
                        pltpu.VMEM((2, s, 2 * HEAD_V), F32),
                        pltpu.VMEM((2, tq1, HEAD_V), F32),
                        pltpu.VMEM((2, tq1, 1), F32),
                        pltpu.VMEM((2, tq1, 1), F32)],
        compiler_params=_params("parallel", "arbitrary"),
        name="l0_attn",
    )(pairs, score_bound, qk, qk, vz, vz, lam_vecs, head_g.reshape(1, HEAD_V), slope_tab)


def _l0_out_l1_in_kernel(o_ref, x_ref, wo_ref, g_ref, wi_ref, x1_ref, u_ref, z_ref):
    e = u_ref.shape[1]
    x1 = x_ref[...] + jnp.dot(o_ref[...], wo_ref[...], preferred_element_type=F32)
    x1_ref[...] = x1
    ms = jnp.mean(x1 * x1, axis=-1, keepdims=True)
    hn = (x1 * lax.rsqrt(ms + EPS) * g_ref[...]).astype(BF16)
    u_ref[...] = jnp.dot(hn, wi_ref[:, :e], preferred_element_type=F32).astype(BF16)
    z_ref[...] = jnp.dot(hn, wi_ref[:, e:], preferred_element_type=F32).astype(BF16)


def _l0_out_l1_in(o, x, wo_bf, norm_g, wi_bf):
    b, s, d = x.shape
    e = o.shape[2]
    tm = _pick(s, 512)
    act_shape = jax.ShapeDtypeStruct((b, s, e), BF16)
    return pl.pallas_call(
        _l0_out_l1_in_kernel,
        out_shape=(jax.ShapeDtypeStruct((b, s, d), F32), act_shape, act_shape),
        grid=(b, s // tm),
        in_specs=[
            pl.BlockSpec((None, tm, e), lambda bi, si: (bi, si, 0)),
            pl.BlockSpec((None, tm, d), lambda bi, si: (bi, si, 0)),
            _const_spec((e, d)),
            _const_spec((1, d)),
            _const_spec((d, 2 * e)),
        ],
        out_specs=(
            pl.BlockSpec((None, tm, d), lambda bi, si: (bi, si, 0)),
            pl.BlockSpec((None, tm, e), lambda bi, si: (bi, si, 0)),
            pl.BlockSpec((None, tm, e), lambda bi, si: (bi, si, 0)),
        ),
        compiler_params=_params("parallel", "parallel"),
        name="l0_out_l1_in",
    )(o, x, wo_bf, norm_g.reshape(1, d), wi_bf)


def _ssm_kernel(u_ref, bt_ref, ct_ref, a_ref, y_ref, il_in_scr, il_out_scr, st0_scr, st1_scr, st2_scr, h_scr):
    batch, n_steps, _ = u_ref.shape
    n_slabs = il_in_scr.shape[0]
    half = batch * n_steps // 2
    c = pl.program_id(1)
    st_scrs = (st0_scr, st1_scr, st2_scr)

    @pl.when(c == 0)
    def _():
        for st in st_scrs:
            st[...] = jnp.zeros(st.shape, F32)
        h_scr[...] = jnp.zeros(h_scr.shape, F32)

    def stages(st_b, st_s, st_c):
        for b in range(batch):
            ub = u_ref[b].astype(F32)
            for k in range(n_slabs):
                il_in_scr[k, pl.ds(b, n_steps, stride=batch), :] = ub[:, k * LANES:(k + 1) * LANES]
        for r in range(2):
            rs = slice(r * half, (r + 1) * half)
            u_il = jnp.concatenate([il_in_scr[k, rs, :] for k in range(n_slabs)], axis=1).astype(BF16)
            st_b[rs, :] = jnp.dot(u_il, bt_ref[...], preferred_element_type=F32)

        for w in range(SSM_TILE_STATE // SCAN_LANES):
            re = pl.ds(w * SCAN_LANES, SCAN_LANES)
            im = pl.ds(SSM_TILE_STATE + w * SCAN_LANES, SCAN_LANES)
            a_re = jnp.broadcast_to(a_ref[0:1, re], (batch, SCAN_LANES))
            a_im = jnp.broadcast_to(a_ref[1:2, re], (batch, SCAN_LANES))
            h_re, h_im = h_scr[:, re], h_scr[:, im]
            for t in range(n_steps):
                rows = pl.ds(t * batch, batch)
                h_re, h_im = (a_re * h_re - a_im * h_im + st_s[rows, re],
                              a_re * h_im + a_im * h_re + st_s[rows, im])
                st_s[rows, re] = h_re
                st_s[rows, im] = h_im
            h_scr[:, re] = h_re
            h_scr[:, im] = h_im

        for r in range(2):
            rs = slice(r * half, (r + 1) * half)
            y = jnp.dot(st_c[rs, :].astype(BF16), ct_ref[...], preferred_element_type=F32)
            for k in range(n_slabs):
                il_out_scr[k, rs, :] = y[:, k * LANES:(k + 1) * LANES]
        for b in range(batch):
            for k in range(n_slabs):
                y_ref[b, :, k * LANES:(k + 1) * LANES] = il_out_scr[
                    k, pl.ds(b, n_steps, stride=batch), :].astype(BF16)

    for res in range(SSM_STAGES):
        pl.when(c % SSM_STAGES == res)(functools.partial(
            stages, st_scrs[res], st_scrs[(res + 2) % SSM_STAGES], st_scrs[(res + 1) % SSM_STAGES]))


def _l1_ssm(u, bt, ct, a_tiles):
    batch, s, e = u.shape
    n_tiles = e // SSM_TILE_CH
    steps = _pick(s, 128)
    rows = steps * batch
    n_chunks = s // steps
    blk_shape = (batch, steps, SSM_TILE_CH)
    st_scratch = pltpu.VMEM((rows, 2 * SSM_TILE_STATE), F32)
    il_scratch = pltpu.VMEM((SSM_TILE_CH // LANES, rows, LANES), F32)
    return pl.pallas_call(
        _ssm_kernel,
        out_shape=jax.ShapeDtypeStruct((batch, s, e), BF16),
        grid=(n_tiles, n_chunks + SSM_STAGES - 1),
        in_specs=[
            pl.BlockSpec(blk_shape, lambda i, c: (0, jnp.minimum(c, n_chunks - 1), i)),
            pl.BlockSpec((None, SSM_TILE_CH, 2 * SSM_TILE_STATE), lambda i, c: (i, 0, 0)),
            pl.BlockSpec((None, 2 * SSM_TILE_STATE, SSM_TILE_CH), lambda i, c: (i, 0, 0)),
            pl.BlockSpec((None, 2, SSM_TILE_STATE), lambda i, c: (i, 0, 0)),
        ],
        out_specs=pl.BlockSpec(blk_shape, lambda i, c: (0, jnp.maximum(c - (SSM_STAGES - 1), 0), i)),
        scratch_shapes=[il_scratch, il_scratch, st_scratch, st_scratch, st_scratch,
                        pltpu.VMEM((batch, 2 * SSM_TILE_STATE), F32)],
        compiler_params=_params("parallel", "arbitrary"),
        name="l1_ssm",
    )(u, bt, ct, a_tiles)


def _glu_out_kernel(y_ref, u_ref, z_ref, x1_ref, d_ref, wg_ref, bg_ref, wo_ref, o_ref):
    rows = y_ref.shape[0] // GLU_ROW_SPLITS
    for r in range(GLU_ROW_SPLITS):
        rs = slice(r * rows, (r + 1) * rows)
        y = y_ref[rs, :].astype(F32) + d_ref[...] * u_ref[rs, :].astype(F32)
        cdf = 0.5 * (1.0 + jnp.tanh(math.sqrt(2.0 / math.pi) * (y + 0.044715 * (y * y * y))))
        y = y * cdf
        gate = jnp.dot(y.astype(BF16), wg_ref[...], preferred_element_type=F32) + bg_ref[...]
        y = y * (1.0 / (1.0 + jnp.exp(-gate)))
        z = z_ref[rs, :].astype(F32)
        y = y * (z / (1.0 + jnp.exp(-z)))
        o_ref[rs, :] = x1_ref[rs, :] + jnp.dot(y.astype(BF16), wo_ref[...], preferred_element_type=F32)


def _l1_glu_out(y, u, z, x1, d_skip, wg_bf, b_glu, wo_bf):
    b, s, d = x1.shape
    e = wg_bf.shape[0]
    ts = _pick(s, GLU_ROWS)
    act_spec = pl.BlockSpec((None, ts, e), lambda bi, si: (bi, si, 0))
    return pl.pallas_call(
        _glu_out_kernel,
        out_shape=jax.ShapeDtypeStruct((b, s, d), F32),
        grid=(b, s // ts),
        in_specs=[act_spec, act_spec, act_spec,
                  pl.BlockSpec((None, ts, d), lambda bi, si: (bi, si, 0)),
                  _const_spec((1, e)), _const_spec((e, e)), _const_spec((1, e)),
                  _const_spec((e, d))],
        out_specs=pl.BlockSpec((None, ts, d), lambda bi, si: (bi, si, 0)),
        compiler_params=_params("parallel", "parallel"),
        name="l1_glu_out",
    )(y, u, z, x1, d_skip.reshape(1, e), wg_bf, b_glu.reshape(1, e), wo_bf)


def _s5_tiles(lam_re, lam_im, log_dt, b_re, b_im, c_re, c_im):
    g, p = lam_re.shape
    ch = b_re.shape[2]
    nt = g // SSM_TILE_GROUPS
    dt = jnp.exp(log_dt.astype(F32))[:, None]
    lr, li = lam_re.astype(F32), lam_im.astype(F32)
    mag = jnp.exp(lr * dt)
    ab_re, ab_im = mag * jnp.cos(li * dt), mag * jnp.sin(li * dt)
    den = lr * lr + li * li
    nr, ni = ab_re - 1.0, ab_im
    g_re = (nr * lr + ni * li) / den
    g_im = (ni * lr - nr * li) / den
    br, bi = b_re.astype(F32), b_im.astype(F32)
    bb_re = g_re[..., None] * br - g_im[..., None] * bi
    bb_im = g_re[..., None] * bi + g_im[..., None] * br
    eye = jnp.eye(SSM_TILE_GROUPS, dtype=F32)

    def b_tile(bb):
        t = bb.reshape(nt, SSM_TILE_GROUPS, p, ch)
        t = jnp.einsum("ngpc,gh->ngchp", t, eye)
        return t.reshape(nt, SSM_TILE_GROUPS * ch, SSM_TILE_GROUPS * p)

    def c_tile(cc):
        t = cc.reshape(nt, SSM_TILE_GROUPS, ch, p)
        t = jnp.einsum("ngcp,gh->ngphc", t, eye)
        return t.reshape(nt, SSM_TILE_GROUPS * p, SSM_TILE_GROUPS * ch)

    bt = jnp.concatenate([b_tile(bb_re), b_tile(bb_im)], axis=2).astype(BF16)
    ct = jnp.concatenate([c_tile(c_re.astype(F32)), -c_tile(c_im.astype(F32))], axis=1).astype(BF16)
    a_tiles = jnp.stack([ab_re.reshape(nt, SSM_TILE_STATE), ab_im.reshape(nt, SSM_TILE_STATE)], axis=1)
    return bt, ct, a_tiles


def kernel(x, l0_norm_g, l0_w_in, l0_q_norm_g, l0_k_norm_g, l0_lam_q1, l0_lam_k1, l0_lam_q2, l0_lam_k2, l0_head_norm_g, l0_w_out, l1_norm_g, l1_w_in, l1_lam_re, l1_lam_im, l1_log_dt, l1_b_re, l1_b_im, l1_c_re, l1_c_im, l1_d, l1_w_glu, l1_b_glu, l1_w_out):
    b, s, d = x.shape
    e = l0_w_out.shape[0]
    n_heads = e // HEAD_V
    assert s % LANES == 0 and e % MXU_DIM == 0 and b == SUBLANES

    gq = l0_q_norm_g.astype(F32) * (SUB_HEAD ** -0.5 * LOG2E)
    gk = l0_k_norm_g.astype(F32)
    reps = e // SUB_HEAD
    col_scale = jnp.concatenate([jnp.tile(gq, reps), jnp.tile(gk, reps)]).reshape(1, 2 * e)
    qk, vz = _l0_inproj(x.reshape(b * s, d), l0_norm_g.astype(F32), l0_w_in.astype(BF16), col_scale)

    score_bound = (BF16_ROUNDING_SLACK * SUB_HEAD * jnp.max(jnp.abs(gq)) * jnp.max(jnp.abs(gk))).reshape(1)
    lam_vecs = jnp.stack([l0_lam_q1, l0_lam_k1, l0_lam_q2, l0_lam_k2]).astype(F32)
    o = _l0_attn(qk.reshape(b, s, 2 * e), vz.reshape(b, s, 2 * e), lam_vecs,
                 l0_head_norm_g.astype(F32), score_bound, n_heads)

    x1, u, z = _l0_out_l1_in(o, x, l0_w_out.astype(BF16), l1_norm_g.astype(F32),
                             l1_w_in.astype(BF16))

    bt, ct, a_tiles = _s5_tiles(l1_lam_re, l1_lam_im, l1_log_dt, l1_b_re, l1_b_im, l1_c_re, l1_c_im)
    y = _l1_ssm(u, bt, ct, a_tiles)

    return _l1_glu_out(y, u, z, x1, l1_d.astype(F32), l1_w_glu.astype(BF16),
                       l1_b_glu.astype(F32), l1_w_out.astype(BF16))
```

```python
import functools
import math

import jax
import jax.numpy as jnp
import numpy as np
from jax import lax
from jax.experimental import pallas as pl
from jax.experimental.pallas import tpu as pltpu

F32 = jnp.float32
BF16 = jnp.bfloat16

EPS = 1e-6
SUB_HEAD = 64
HEAD_V = 2 * SUB_HEAD
GROUP_CH = 16
GROUP_STATE = 64
LAMBDA_INIT_L0 = 0.8 - 0.6 * math.exp(-0.3 * 0)
LOG2E = math.log2(math.e)

LANES = 128
SUBLANES = 8
MXU_DIM = 256
VMEM_LIMIT_BYTES = 56 * 1024 * 1024
NEG_BIG = -1e30
F32_MIN_EXP = 126

INPROJ_TN = 1024
ATTN_TQ = 512
ATTN_TQ_ONLINE = 256
FIXED_SHIFT_MAX_LOG2 = (F32_MIN_EXP - 26) / 2.0
BF16_ROUNDING_SLACK = 1.01

SSM_TILE_CH = MXU_DIM
SSM_TILE_GROUPS = SSM_TILE_CH // GROUP_CH
SSM_TILE_STATE = SSM_TILE_GROUPS * GROUP_STATE
SCAN_LANES = 256
GLU_ROWS = 512
GLU_ROW_SPLITS = 2
SSM_STAGES = 3


def _params(*sem):
    return pltpu.CompilerParams(dimension_semantics=sem, vmem_limit_bytes=VMEM_LIMIT_BYTES)


def _const_spec(shape):
    nd = len(shape)
    return pl.BlockSpec(shape, lambda *_: (0,) * nd, pipeline_mode=pl.Buffered(1))


def _pick(n, pref):
    t = min(n, pref)
    while n % t:
        t //= 2
    return t


def _l0_inproj_kernel(x_ref, g_ref, wqk_ref, wvz_ref, cs_ref, e_ref, qk_ref, vz_ref, h_scr):
    @pl.when(pl.program_id(1) == 0)
    def _():
        xf = x_ref[...]
        ms = jnp.mean(xf * xf, axis=-1, keepdims=True)
        h_scr[...] = (xf * lax.rsqrt(ms + EPS) * g_ref[...]).astype(BF16)

    hn = h_scr[...]
    acc = jnp.dot(hn, wqk_ref[...], preferred_element_type=F32)
    sq = acc * acc
    hi = sq.astype(BF16)
    lo = (sq - hi.astype(F32)).astype(BF16)
    e = e_ref[...]
    for c in range(acc.shape[1] // MXU_DIM):
        sl = slice(c * MXU_DIM, (c + 1) * MXU_DIM)
        ss = (jnp.dot(hi[:, sl], e, preferred_element_type=F32)
              + jnp.dot(lo[:, sl], e, preferred_element_type=F32))
        inv = lax.rsqrt(ss * (1.0 / SUB_HEAD) + EPS)
        qk_ref[:, sl] = (acc[:, sl] * inv * cs_ref[:, sl]).astype(BF16)
    vz_ref[...] = jnp.dot(hn, wvz_ref[...], preferred_element_type=F32).astype(BF16)


def _l0_inproj(x2d, norm_g, w_bf, col_scale):
    m, d = x2d.shape
    n_half = w_bf.shape[1] // 2
    tm = _pick(m, 1024)
    tn = _pick(n_half, INPROJ_TN)
    n_col_tiles = n_half // tn
    seg = np.arange(MXU_DIM) // SUB_HEAD
    e = jnp.asarray(seg[:, None] == seg[None, :], dtype=BF16)
    out = jax.ShapeDtypeStruct((m, n_half), BF16)
    return pl.pallas_call(
        _l0_inproj_kernel,
        out_shape=(out, out),
        grid=(m // tm, n_col_tiles),
        in_specs=[
            pl.BlockSpec((tm, d), lambda i, j: (i, 0)),
            _const_spec((1, d)),
            pl.BlockSpec((d, tn), lambda i, j: (0, j)),
            pl.BlockSpec((d, tn), lambda i, j: (0, j + n_col_tiles)),
            pl.BlockSpec((1, tn), lambda i, j: (0, j)),
            _const_spec((MXU_DIM, MXU_DIM)),
        ],
        out_specs=(pl.BlockSpec((tm, tn), lambda i, j: (i, j)),
                   pl.BlockSpec((tm, tn), lambda i, j: (i, j))),
        scratch_shapes=[pltpu.VMEM((tm, d), BF16)],
        compiler_params=_params("parallel", "arbitrary"),
        name="l0_inproj",
    )(x2d, norm_g.reshape(1, d), w_bf, w_bf, col_scale, e)


def _split3(val):
    p1 = val.astype(BF16).astype(F32)
    r1 = val - p1
    p2 = r1.astype(BF16).astype(F32)
    p3 = (r1 - p2).astype(BF16).astype(F32)
    return p1, p2, p3


def _augment(parts, lane, off, ones_first):
    p_off = off + 3 if ones_first else off
    o_off = off if ones_first else off + 3
    a = jnp.where(lane == p_off, parts[0], 0.0)
    a = jnp.where(lane == p_off + 1, parts[1], a)
    a = jnp.where(lane == p_off + 2, parts[2], a)
    for t in range(3):
        a = jnp.where(lane == o_off + t, 1.0, a)
    return a


def _scores(qa, kk):
    return lax.dot_general(qa, kk, (((1,), (1,)), ((), ())), preferred_element_type=F32)


def _attn_kernel(shift_ref, q_ref, k_ref, v_ref, z_ref, lam_ref, hg_ref, slope_ref, o_ref,
                 aq_scr, ak_scr, qaug_scr, kaug_scr, vaug_scr, acc_scr, acc1_scr, m_scr, l_scr,
                 *, tq, tq1):
    s_len = q_ref.shape[0]
    n_blk = s_len // tq
    half = tq // 2
    h = pl.program_id(0)
    shift = shift_ref[0]
    slope = slope_ref[pl.ds(h, 1), :] * LOG2E
    lane = lax.broadcasted_iota(jnp.int32, (tq, LANES), 1)
    row = lax.broadcasted_iota(jnp.int32, (tq, LANES), 0)

    lam_v = lam_ref[...]
    lam = (jnp.exp(jnp.sum(lam_v[0:1] * lam_v[1:2], axis=1, keepdims=True))
           - jnp.exp(jnp.sum(lam_v[2:3] * lam_v[3:4], axis=1, keepdims=True)) + LAMBDA_INIT_L0)

    def blk(i, size):
        return pl.ds(pl.multiple_of(i * size, size), size)

    def finish(rows, o0, l0, o1, l1):
        o = o0 * (1.0 / l0) - lam * (o1 * (1.0 / l1))
        ms = jnp.mean(o * o, axis=-1, keepdims=True)
        on = o * lax.rsqrt(ms + EPS) * hg_ref[...] * (1.0 - LAMBDA_INIT_L0)
        z = z_ref[rows, :].astype(F32)
        o_ref[rows, :] = (on * (z / (1.0 + jnp.exp(-z)))).astype(BF16)

    @pl.when(pl.program_id(1) == 0)
    def _tables():
        ones_col = jnp.ones((tq, HEAD_V), BF16)

        def body(c, carry):
            rows = blk(c, tq)
            pos = slope * (c * tq + row).astype(F32)
            k_parts = _split3(pos)
            q_parts = _split3(-(pos + shift))
            ak_scr[0, rows, :] = _augment(k_parts, lane, SUB_HEAD, True).astype(BF16)
            ak_scr[1, rows, :] = _augment(k_parts, lane, 0, True).astype(BF16)
            aq_scr[0, rows, :] = _augment(q_parts, lane, SUB_HEAD, False).astype(BF16)
            aq_scr[1, rows, :] = _augment(q_parts, lane, 0, False).astype(BF16)
            vaug_scr[rows, HEAD_V:2 * HEAD_V] = ones_col
            return carry

        lax.fori_loop(0, n_blk, body, 0)

    lane_row = lax.broadcasted_iota(jnp.int32, (1, LANES), 1)
    keep = (jnp.where(lane_row < SUB_HEAD, 1.0, 0.0).astype(BF16),
            jnp.where(lane_row < SUB_HEAD, 0.0, 1.0).astype(BF16))

    def build(rows):
        kb = k_ref[rows, :]
        qb = q_ref[rows, :]
        for sub in range(2):
            kaug_scr[sub, rows, :] = kb * keep[sub] + ak_scr[sub, rows, :]
            qaug_scr[sub, rows, :] = qb * keep[sub] + aq_scr[sub, rows, :]
        vaug_scr[rows, 0:HEAD_V] = v_ref[rows, :]

    fixed_shift_ok = shift <= FIXED_SHIFT_MAX_LOG2

    @pl.when(fixed_shift_ok)
    def _fixed_shift():
        def tile(q_rows, k_rows, mask_off, init):
            n_q, n_k = q_rows.size, k_rows.size
            for sub in range(2):
                s = _scores(qaug_scr[sub, q_rows, :], kaug_scr[sub, k_rows, :])
                if mask_off is not None:
                    cols = lax.broadcasted_iota(jnp.int32, (n_q, n_k), 1)
                    rws = lax.broadcasted_iota(jnp.int32, (n_q, n_k), 0)
                    s = jnp.where(cols <= rws + mask_off, s, -jnp.inf)
                pv = jnp.dot(jnp.exp2(s).astype(BF16), vaug_scr[k_rows, :], preferred_element_type=F32)
                if init:
                    acc_scr[sub, q_rows, :] = pv
                else:
                    acc_scr[sub, q_rows, :] += pv

        for i in range(n_blk):
            r0 = i * tq
            build(pl.ds(r0, tq))
            tile(pl.ds(r0, half), pl.ds(r0, half), 0, True)
            tile(pl.ds(r0 + half, half), pl.ds(r0, tq), half, True)
            for j in range(i):
                tile(pl.ds(r0, tq), pl.ds(j * tq, tq), None, False)
            rows = pl.ds(r0, tq)
            a0 = acc_scr[0, rows, :]
            a1 = acc_scr[1, rows, :]
            finish(rows, a0[:, :HEAD_V], a0[:, HEAD_V:], a1[:, :HEAD_V], a1[:, HEAD_V:])

    @pl.when(jnp.logical_not(fixed_shift_ok))
    def _online():
        col_s = lax.broadcasted_iota(jnp.int32, (tq1, tq1), 1)
        row_s = lax.broadcasted_iota(jnp.int32, (tq1, tq1), 0)
        causal = col_s <= row_s

        def build_block(c, carry):
            build(blk(c, tq))
            return carry

        lax.fori_loop(0, n_blk, build_block, 0)

        def q_block(i, carry):
            rows = blk(i, tq1)
            m_scr[...] = jnp.full(m_scr.shape, NEG_BIG, F32)
            l_scr[...] = jnp.zeros(l_scr.shape, F32)
            acc1_scr[...] = jnp.zeros(acc1_scr.shape, F32)

            def kv_step(j, masked):
                k_rows = blk(j, tq1)
                v = v_ref[k_rows, :]
                for sub in range(2):
                    s = _scores(qaug_scr[sub, rows, :], kaug_scr[sub, k_rows, :])
                    if masked:
                        s = jnp.where(causal, s, -jnp.inf)
                    m_prev = m_scr[sub]
                    m_new = jnp.maximum(m_prev, jnp.max(s, axis=1, keepdims=True))
                    alpha = jnp.exp2(m_prev - m_new)
                    p = jnp.exp2(s - m_new)
                    l_scr[sub] = alpha * l_scr[sub] + jnp.sum(p, axis=1, keepdims=True)
                    acc1_scr[sub] = alpha * acc1_scr[sub] + jnp.dot(
                        p.astype(BF16), v, preferred_element_type=F32)
                    m_scr[sub] = m_new

            def off_diag(j, c):
                kv_step(j, False)
                return c

            lax.fori_loop(0, i, off_diag, 0)
            kv_step(i, True)
            finish(rows, acc1_scr[0], l_scr[0], acc1_scr[1], l_scr[1])
            return carry

        lax.fori_loop(0, s_len // tq1, q_block, 0)


def _l0_attn(qk, vz, lam_vecs, head_g, score_bound, n_heads):
    b, s, _ = qk.shape
    tq = _pick(s, ATTN_TQ)
    tq1 = _pick(s, ATTN_TQ_ONLINE)
    slopes = 2.0 ** (-8.0 * np.arange(1, n_heads + 1) / n_heads)
    slope_tab = jnp.asarray(np.broadcast_to(slopes[:, None], (n_heads, LANES)), dtype=F32)

    def head_block(which):
        return pl.BlockSpec((None, s, HEAD_V), lambda hi, bi: (bi, 0, which * n_heads + hi))

    smem = pl.BlockSpec(memory_space=pltpu.SMEM)
    return pl.pallas_call(
        functools.partial(_attn_kernel, tq=tq, tq1=tq1),
        out_shape=jax.ShapeDtypeStruct((b, s, n_heads * HEAD_V), BF16),
        grid=(n_heads, b),
        in_specs=[smem, head_block(0), head_block(1), head_block(0), head_block(1),
                  _const_spec((4, SUB_HEAD)), _const_spec((1, HEAD_V)),
                  _const_spec((n_heads, LANES))],
        out_specs=pl.BlockSpec((None, s, HEAD_V), lambda hi, bi: (bi, 0, hi)),
        scratch_shapes=[pltpu.VMEM((2, s, HEAD_V), BF16),
                        pltpu.VMEM((2, s, HEAD_V), BF16),
                        pltpu.VMEM((2, s, HEAD_V), BF16),
                        pltpu.VMEM((2, s, HEAD_V), BF16),
                        pltpu.VMEM((s, 2 * HEAD_V), BF16),
                        pltpu.VMEM((2, s, 2 * HEAD_V), F32),
                        pltpu.VMEM((2, tq1, HEAD_V), F32),
                        pltpu.VMEM((2, tq1, 1), F32),
                        pltpu.VMEM((2, tq1, 1), F32)],
        compiler_params=_params("parallel", "arbitrary"),
        name="l0_attn",
    )(score_bound, qk, qk, vz, vz, lam_vecs, head_g.reshape(1, HEAD_V), slope_tab)


def _l0_out_l1_in_kernel(o_ref, x_ref, wo_ref, g_ref, wi_ref, x1_ref, u_ref, z_ref):
    e = u_ref.shape[1]
    x1 = x_ref[...] + jnp.dot(o_ref[...], wo_ref[...], preferred_element_type=F32)
    x1_ref[...] = x1
    ms = jnp.mean(x1 * x1, axis=-1, keepdims=True)
    hn = (x1 * lax.rsqrt(ms + EPS) * g_ref[...]).astype(BF16)
    u_ref[...] = jnp.dot(hn, wi_ref[:, :e], preferred_element_type=F32).astype(BF16)
    z_ref[...] = jnp.dot(hn, wi_ref[:, e:], preferred_element_type=F32).astype(BF16)


def _l0_out_l1_in(o, x, wo_bf, norm_g, wi_bf):
    b, s, d = x.shape
    e = o.shape[2]
    tm = _pick(s, 512)
    act_shape = jax.ShapeDtypeStruct((b, s, e), BF16)
    return pl.pallas_call(
        _l0_out_l1_in_kernel,
        out_shape=(jax.ShapeDtypeStruct((b, s, d), F32), act_shape, act_shape),
        grid=(b, s // tm),
        in_specs=[
            pl.BlockSpec((None, tm, e), lambda bi, si: (bi, si, 0)),
            pl.BlockSpec((None, tm, d), lambda bi, si: (bi, si, 0)),
            _const_spec((e, d)),
            _const_spec((1, d)),
            _const_spec((d, 2 * e)),
        ],
        out_specs=(
            pl.BlockSpec((None, tm, d), lambda bi, si: (bi, si, 0)),
            pl.BlockSpec((None, tm, e), lambda bi, si: (bi, si, 0)),
            pl.BlockSpec((None, tm, e), lambda bi, si: (bi, si, 0)),
        ),
        compiler_params=_params("parallel", "parallel"),
        name="l0_out_l1_in",
    )(o, x, wo_bf, norm_g.reshape(1, d), wi_bf)


def _ssm_kernel(u_ref, bt_ref, ct_ref, a_ref, y_ref, il_in_scr, il_out_scr, st0_scr, st1_scr, st2_scr, h_scr):
    batch, n_steps, _ = u_ref.shape
    n_slabs = il_in_scr.shape[0]
    half = batch * n_steps // 2
    c = pl.program_id(1)
    st_scrs = (st0_scr, st1_scr, st2_scr)

    @pl.when(c == 0)
    def _():
        for st in st_scrs:
            st[...] = jnp.zeros(st.shape, F32)
        h_scr[...] = jnp.zeros(h_scr.shape, F32)

    def stages(st_b, st_s, st_c):
        for b in range(batch):
            ub = u_ref[b].astype(F32)
            for k in range(n_slabs):
                il_in_scr[k, pl.ds(b, n_steps, stride=batch), :] = ub[:, k * LANES:(k + 1) * LANES]
        for r in range(2):
            rs = slice(r * half, (r + 1) * half)
            u_il = jnp.concatenate([il_in_scr[k, rs, :] for k in range(n_slabs)], axis=1).astype(BF16)
            st_b[rs, :] = jnp.dot(u_il, bt_ref[...], preferred_element_type=F32)

        for w in range(SSM_TILE_STATE // SCAN_LANES):
            re = pl.ds(w * SCAN_LANES, SCAN_LANES)
            im = pl.ds(SSM_TILE_STATE + w * SCAN_LANES, SCAN_LANES)
            a_re = jnp.broadcast_to(a_ref[0:1, re], (batch, SCAN_LANES))
            a_im = jnp.broadcast_to(a_ref[1:2, re], (batch, SCAN_LANES))
            h_re, h_im = h_scr[:, re], h_scr[:, im]
            for t in range(n_steps):
                rows = pl.ds(t * batch, batch)
                h_re, h_im = (a_re * h_re - a_im * h_im + st_s[rows, re],
                              a_re * h_im + a_im * h_re + st_s[rows, im])
                st_s[rows, re] = h_re
                st_s[rows, im] = h_im
            h_scr[:, re] = h_re
            h_scr[:, im] = h_im

        for r in range(2):
            rs = slice(r * half, (r + 1) * half)
            y = jnp.dot(st_c[rs, :].astype(BF16), ct_ref[...], preferred_element_type=F32)
            for k in range(n_slabs):
                il_out_scr[k, rs, :] = y[:, k * LANES:(k + 1) * LANES]
        for b in range(batch):
            for k in range(n_slabs):
                y_ref[b, :, k * LANES:(k + 1) * LANES] = il_out_scr[
                    k, pl.ds(b, n_steps, stride=batch), :].astype(BF16)

    for res in range(SSM_STAGES):
        pl.when(c % SSM_STAGES == res)(functools.partial(
            stages, st_scrs[res], st_scrs[(res + 2) % SSM_STAGES], st_scrs[(res + 1) % SSM_STAGES]))


def _l1_ssm(u, bt, ct, a_tiles):
    batch, s, e = u.shape
    n_tiles = e // SSM_TILE_CH
    steps = _pick(s, 128)
    rows = steps * batch
    n_chunks = s // steps
    blk_shape = (batch, steps, SSM_TILE_CH)
    st_scratch = pltpu.VMEM((rows, 2 * SSM_TILE_STATE), F32)
    il_scratch = pltpu.VMEM((SSM_TILE_CH // LANES, rows, LANES), F32)
    return pl.pallas_call(
        _ssm_kernel,
        out_shape=jax.ShapeDtypeStruct((batch, s, e), BF16),
        grid=(n_tiles, n_chunks + SSM_STAGES - 1),
        in_specs=[
            pl.BlockSpec(blk_shape, lambda i, c: (0, jnp.minimum(c, n_chunks - 1), i)),
            pl.BlockSpec((None, SSM_TILE_CH, 2 * SSM_TILE_STATE), lambda i, c: (i, 0, 0)),
            pl.BlockSpec((None, 2 * SSM_TILE_STATE, SSM_TILE_CH), lambda i, c: (i, 0, 0)),
            pl.BlockSpec((None, 2, SSM_TILE_STATE), lambda i, c: (i, 0, 0)),
        ],
        out_specs=pl.BlockSpec(blk_shape, lambda i, c: (0, jnp.maximum(c - (SSM_STAGES - 1), 0), i)),
        scratch_shapes=[il_scratch, il_scratch, st_scratch, st_scratch, st_scratch,
                        pltpu.VMEM((batch, 2 * SSM_TILE_STATE), F32)],
        compiler_params=_params("parallel", "arbitrary"),
        name="l1_ssm",
    )(u, bt, ct, a_tiles)


def _glu_out_kernel(y_ref, u_ref, z_ref, x1_ref, d_ref, wg_ref, bg_ref, wo_ref, o_ref):
    rows = y_ref.shape[0] // GLU_ROW_SPLITS
    for r in range(GLU_ROW_SPLITS):
        rs = slice(r * rows, (r + 1) * rows)
        y = y_ref[rs, :].astype(F32) + d_ref[...] * u_ref[rs, :].astype(F32)
        cdf = 0.5 * (1.0 + jnp.tanh(math.sqrt(2.0 / math.pi) * (y + 0.044715 * (y * y * y))))
        y = y * cdf
        gate = jnp.dot(y.astype(BF16), wg_ref[...], preferred_element_type=F32) + bg_ref[...]
        y = y * (1.0 / (1.0 + jnp.exp(-gate)))
        z = z_ref[rs, :].astype(F32)
        y = y * (z / (1.0 + jnp.exp(-z)))
        o_ref[rs, :] = x1_ref[rs, :] + jnp.dot(y.astype(BF16), wo_ref[...], preferred_element_type=F32)


def _l1_glu_out(y, u, z, x1, d_skip, wg_bf, b_glu, wo_bf):
    b, s, d = x1.shape
    e = wg_bf.shape[0]
    ts = _pick(s, GLU_ROWS)
    act_spec = pl.BlockSpec((None, ts, e), lambda bi, si: (bi, si, 0))
    return pl.pallas_call(
        _glu_out_kernel,
        out_shape=jax.ShapeDtypeStruct((b, s, d), F32),
        grid=(b, s // ts),
        in_specs=[act_spec, act_spec, act_spec,
                  pl.BlockSpec((None, ts, d), lambda bi, si: (bi, si, 0)),
                  _const_spec((1, e)), _const_spec((e, e)), _const_spec((1, e)),
                  _const_spec((e, d))],
        out_specs=pl.BlockSpec((None, ts, d), lambda bi, si: (bi, si, 0)),
        compiler_params=_params("parallel", "parallel"),
        name="l1_glu_out",
    )(y, u, z, x1, d_skip.reshape(1, e), wg_bf, b_glu.reshape(1, e), wo_bf)


def _s5_tiles(lam_re, lam_im, log_dt, b_re, b_im, c_re, c_im):
    g, p = lam_re.shape
    ch = b_re.shape[2]
    nt = g // SSM_TILE_GROUPS
    dt = jnp.exp(log_dt.astype(F32))[:, None]
    lr, li = lam_re.astype(F32), lam_im.astype(F32)
    mag = jnp.exp(lr * dt)
    ab_re, ab_im = mag * jnp.cos(li * dt), mag * jnp.sin(li * dt)
    den = lr * lr + li * li
    nr, ni = ab_re - 1.0, ab_im
    g_re = (nr * lr + ni * li) / den
    g_im = (ni * lr - nr * li) / den
    br, bi = b_re.astype(F32), b_im.astype(F32)
    bb_re = g_re[..., None] * br - g_im[..., None] * bi
    bb_im = g_re[..., None] * bi + g_im[..., None] * br
    eye = jnp.eye(SSM_TILE_GROUPS, dtype=F32)

    def b_tile(bb):
        t = bb.reshape(nt, SSM_TILE_GROUPS, p, ch)
        t = jnp.einsum("ngpc,gh->ngchp", t, eye)
        return t.reshape(nt, SSM_TILE_GROUPS * ch, SSM_TILE_GROUPS * p)

    def c_tile(cc):
        t = cc.reshape(nt, SSM_TILE_GROUPS, ch, p)
        t = jnp.einsum("ngcp,gh->ngphc", t, eye)
        return t.reshape(nt, SSM_TILE_GROUPS * p, SSM_TILE_GROUPS * ch)

    bt = jnp.concatenate([b_tile(bb_re), b_tile(bb_im)], axis=2).astype(BF16)
    ct = jnp.concatenate([c_tile(c_re.astype(F32)), -c_tile(c_im.astype(F32))], axis=1).astype(BF16)
    a_tiles = jnp.stack([ab_re.reshape(nt, SSM_TILE_STATE), ab_im.reshape(nt, SSM_TILE_STATE)], axis=1)
    return bt, ct, a_tiles


def kernel(x, l0_norm_g, l0_w_in, l0_q_norm_g, l0_k_norm_g, l0_lam_q1, l0_lam_k1, l0_lam_q2, l0_lam_k2, l0_head_norm_g, l0_w_out, l1_norm_g, l1_w_in, l1_lam_re, l1_lam_im, l1_log_dt, l1_b_re, l1_b_im, l1_c_re, l1_c_im, l1_d, l1_w_glu, l1_b_glu, l1_w_out):
    b, s, d = x.shape
    e = l0_w_out.shape[0]
    n_heads = e // HEAD_V
    assert s % LANES == 0 and e % MXU_DIM == 0 and b == SUBLANES

    gq = l0_q_norm_g.astype(F32) * (SUB_HEAD ** -0.5 * LOG2E)
    gk = l0_k_norm_g.astype(F32)
    reps = e // SUB_HEAD
    col_scale = jnp.concatenate([jnp.tile(gq, reps), jnp.tile(gk, reps)]).reshape(1, 2 * e)
    qk, vz = _l0_inproj(x.reshape(b * s, d), l0_norm_g.astype(F32), l0_w_in.astype(BF16), col_scale)

    score_bound = (BF16_ROUNDING_SLACK * SUB_HEAD * jnp.max(jnp.abs(gq)) * jnp.max(jnp.abs(gk))).reshape(1)
    lam_vecs = jnp.stack([l0_lam_q1, l0_lam_k1, l0_lam_q2, l0_lam_k2]).astype(F32)
    o = _l0_attn(qk.reshape(b, s, 2 * e), vz.reshape(b, s, 2 * e), lam_vecs,
                 l0_head_norm_g.astype(F32), score_bound, n_heads)

    x1, u, z = _l0_out_l1_in(o, x, l0_w_out.astype(BF16), l1_norm_g.astype(F32),
                             l1_w_in.astype(BF16))

    bt, ct, a_tiles = _s5_tiles(l1_lam_re, l1_lam_im, l1_log_dt, l1_b_re, l1_b_im, l1_c_re, l1_c_im)
    y = _l1_ssm(u, bt, ct, a_tiles)

    return _l1_glu_out(y, u, z, x1, l1_d.astype(F32), l1_w_glu.astype(BF16),
                       l1_b_glu.astype(F32), l1_w_out.astype(BF16))
```

```python
import functools
import math

import jax
import jax.numpy as jnp
import numpy as np
from jax import lax
from jax.experimental import pallas as pl
from jax.experimental.pallas import tpu as pltpu

F32 = jnp.float32
BF16 = jnp.bfloat16

EPS = 1e-6
SUB_HEAD = 64
HEAD_V = 2 * SUB_HEAD
GROUP_CH = 16
GROUP_STATE = 64
LAMBDA_INIT_L0 = 0.8 - 0.6 * math.exp(-0.3 * 0)
LOG2E = math.log2(math.e)

LANES = 128
SUBLANES = 8
MXU_DIM = 256
VMEM_LIMIT_BYTES = 56 * 1024 * 1024
NEG_BIG = -1e30
F32_MIN_EXP = 126

INPROJ_TN = 1024
ATTN_TQ = 512
ATTN_TQ_ONLINE = 256
FIXED_SHIFT_MAX_LOG2 = (F32_MIN_EXP - 26) / 2.0
BF16_ROUNDING_SLACK = 1.01

SSM_TILE_CH = MXU_DIM
SSM_TILE_GROUPS = SSM_TILE_CH // GROUP_CH
SSM_TILE_STATE = SSM_TILE_GROUPS * GROUP_STATE
SCAN_LANES = 256
GLU_ROWS = 512
GLU_ROW_SPLITS = 2
SSM_STAGES = 3


def _params(*sem):
    return pltpu.CompilerParams(dimension_semantics=sem, vmem_limit_bytes=VMEM_LIMIT_BYTES)


def _const_spec(shape):
    nd = len(shape)
    return pl.BlockSpec(shape, lambda *_: (0,) * nd, pipeline_mode=pl.Buffered(1))


def _pick(n, pref):
    t = min(n, pref)
    while n % t:
        t //= 2
    return t


def _l0_inproj_kernel(x_ref, g_ref, wqk_ref, wvz_ref, cs_ref, e_ref, qk_ref, vz_ref, h_scr):
    @pl.when(pl.program_id(1) == 0)
    def _():
        xf = x_ref[...]
        ms = jnp.mean(xf * xf, axis=-1, keepdims=True)
        h_scr[...] = (xf * lax.rsqrt(ms + EPS) * g_ref[...]).astype(BF16)

    hn = h_scr[...]
    acc = jnp.dot(hn, wqk_ref[...], preferred_element_type=F32)
    sq = acc * acc
    hi = sq.astype(BF16)
    lo = (sq - hi.astype(F32)).astype(BF16)
    e = e_ref[...]
    for c in range(acc.shape[1] // MXU_DIM):
        sl = slice(c * MXU_DIM, (c + 1) * MXU_DIM)
        ss = (jnp.dot(hi[:, sl], e, preferred_element_type=F32)
              + jnp.dot(lo[:, sl], e, preferred_element_type=F32))
        inv = lax.rsqrt(ss * (1.0 / SUB_HEAD) + EPS)
        qk_ref[:, sl] = (acc[:, sl] * inv * cs_ref[:, sl]).astype(BF16)
    vz_ref[...] = jnp.dot(hn, wvz_ref[...], preferred_element_type=F32).astype(BF16)


def _l0_inproj(x2d, norm_g, w_bf, col_scale):
    m, d = x2d.shape
    n_half = w_bf.shape[1] // 2
    tm = _pick(m, 1024)
    tn = _pick(n_half, INPROJ_TN)
    n_col_tiles = n_half // tn
    seg = np.arange(MXU_DIM) // SUB_HEAD
    e = jnp.asarray(seg[:, None] == seg[None, :], dtype=BF16)
    out = jax.ShapeDtypeStruct((m, n_half), BF16)
    return pl.pallas_call(
        _l0_inproj_kernel,
        out_shape=(out, out),
        grid=(m // tm, n_col_tiles),
        in_specs=[
            pl.BlockSpec((tm, d), lambda i, j: (i, 0)),
            _const_spec((1, d)),
            pl.BlockSpec((d, tn), lambda i, j: (0, j)),
            pl.BlockSpec((d, tn), lambda i, j: (0, j + n_col_tiles)),
            pl.BlockSpec((1, tn), lambda i, j: (0, j)),
            _const_spec((MXU_DIM, MXU_DIM)),
        ],
        out_specs=(pl.BlockSpec((tm, tn), lambda i, j: (i, j)),
                   pl.BlockSpec((tm, tn), lambda i, j: (i, j))),
        scratch_shapes=[pltpu.VMEM((tm, d), BF16)],
        compiler_params=_params("parallel", "arbitrary"),
        name="l0_inproj",
    )(x2d, norm_g.reshape(1, d), w_bf, w_bf, col_scale, e)


def _split3(val):
    p1 = val.astype(BF16).astype(F32)
    r1 = val - p1
    p2 = r1.astype(BF16).astype(F32)
    p3 = (r1 - p2).astype(BF16).astype(F32)
    return p1, p2, p3


def _augment(parts, lane, off, ones_first):
    p_off = off + 3 if ones_first else off
    o_off = off if ones_first else off + 3
    a = jnp.where(lane == p_off, parts[0], 0.0)
    a = jnp.where(lane == p_off + 1, parts[1], a)
    a = jnp.where(lane == p_off + 2, parts[2], a)
    for t in range(3):
        a = jnp.where(lane == o_off + t, 1.0, a)
    return a


def _scores(qa, kk):
    return lax.dot_general(qa, kk, (((1,), (1,)), ((), ())), preferred_element_type=F32)


def _attn_kernel(shift_ref, q_ref, k_ref, v_ref, z_ref, lam_ref, hg_ref, slope_ref, o_ref,
                 aq_scr, ak_scr, qaug_scr, kaug_scr, vaug_scr, acc_scr, acc1_scr, m_scr, l_scr,
                 *, tq, tq1):
    s_len = q_ref.shape[0]
    n_blk = s_len // tq
    half = tq // 2
    h = pl.program_id(0)
    shift = shift_ref[0]
    slope = slope_ref[pl.ds(h, 1), :] * LOG2E
    lane = lax.broadcasted_iota(jnp.int32, (tq, LANES), 1)
    row = lax.broadcasted_iota(jnp.int32, (tq, LANES), 0)

    lam_v = lam_ref[...]
    lam = (jnp.exp(jnp.sum(lam_v[0:1] * lam_v[1:2], axis=1, keepdims=True))
           - jnp.exp(jnp.sum(lam_v[2:3] * lam_v[3:4], axis=1, keepdims=True)) + LAMBDA_INIT_L0)

    def blk(i, size):
        return pl.ds(pl.multiple_of(i * size, size), size)

    def finish(rows, o0, l0, o1, l1):
        o = o0 * (1.0 / l0) - lam * (o1 * (1.0 / l1))
        ms = jnp.mean(o * o, axis=-1, keepdims=True)
        on = o * lax.rsqrt(ms + EPS) * hg_ref[...] * (1.0 - LAMBDA_INIT_L0)
        z = z_ref[rows, :].astype(F32)
        o_ref[rows, :] = (on * (z / (1.0 + jnp.exp(-z)))).astype(BF16)

    @pl.when(pl.program_id(1) == 0)
    def _tables():
        ones_col = jnp.ones((tq, HEAD_V), BF16)

        def body(c, carry):
            rows = blk(c, tq)
            pos = slope * (c * tq + row).astype(F32)
            k_parts = _split3(pos)
            q_parts = _split3(-(pos + shift))
            ak_scr[0, rows, :] = _augment(k_parts, lane, SUB_HEAD, True).astype(BF16)
            ak_scr[1, rows, :] = _augment(k_parts, lane, 0, True).astype(BF16)
            aq_scr[0, rows, :] = _augment(q_parts, lane, SUB_HEAD, False).astype(BF16)
            aq_scr[1, rows, :] = _augment(q_parts, lane, 0, False).astype(BF16)
            vaug_scr[rows, HEAD_V:2 * HEAD_V] = ones_col
            return carry

        lax.fori_loop(0, n_blk, body, 0)

    lane_row = lax.broadcasted_iota(jnp.int32, (1, LANES), 1)
    keep = (jnp.where(lane_row < SUB_HEAD, 1.0, 0.0).astype(BF16),
            jnp.where(lane_row < SUB_HEAD, 0.0, 1.0).astype(BF16))

    def build(rows):
        kb = k_ref[rows, :]
        qb = q_ref[rows, :]
        for sub in range(2):
            kaug_scr[sub, rows, :] = kb * keep[sub] + ak_scr[sub, rows, :]
            qaug_scr[sub, rows, :] = qb * keep[sub] + aq_scr[sub, rows, :]
        vaug_scr[rows, 0:HEAD_V] = v_ref[rows, :]

    fixed_shift_ok = shift <= FIXED_SHIFT_MAX_LOG2

    @pl.when(fixed_shift_ok)
    def _fixed_shift():
        def tile(q_rows, k_rows, mask_off, init):
            n_q, n_k = q_rows.size, k_rows.size
            for sub in range(2):
                s = _scores(qaug_scr[sub, q_rows, :], kaug_scr[sub, k_rows, :])
                if mask_off is not None:
                    cols = lax.broadcasted_iota(jnp.int32, (n_q, n_k), 1)
                    rws = lax.broadcasted_iota(jnp.int32, (n_q, n_k), 0)
                    s = jnp.where(cols <= rws + mask_off, s, -jnp.inf)
                pv = jnp.dot(jnp.exp2(s).astype(BF16), vaug_scr[k_rows, :], preferred_element_type=F32)
                if init:
                    acc_scr[sub, q_rows, :] = pv
                else:
                    acc_scr[sub, q_rows, :] += pv

        for i in range(n_blk):
            r0 = i * tq
            build(pl.ds(r0, tq))
            tile(pl.ds(r0, half), pl.ds(r0, half), 0, True)
            tile(pl.ds(r0 + half, half), pl.ds(r0, tq), half, True)
            for j in range(i):
                tile(pl.ds(r0, tq), pl.ds(j * tq, tq), None, False)
            rows = pl.ds(r0, tq)
            a0 = acc_scr[0, rows, :]
            a1 = acc_scr[1, rows, :]
            finish(rows, a0[:, :HEAD_V], a0[:, HEAD_V:], a1[:, :HEAD_V], a1[:, HEAD_V:])

    @pl.when(jnp.logical_not(fixed_shift_ok))
    def _online():
        col_s = lax.broadcasted_iota(jnp.int32, (tq1, tq1), 1)
        row_s = lax.broadcasted_iota(jnp.int32, (tq1, tq1), 0)
        causal = col_s <= row_s

        def build_block(c, carry):
            build(blk(c, tq))
            return carry

        lax.fori_loop(0, n_blk, build_block, 0)

        def q_block(i, carry):
            rows = blk(i, tq1)
            m_scr[...] = jnp.full(m_scr.shape, NEG_BIG, F32)
            l_scr[...] = jnp.zeros(l_scr.shape, F32)
            acc1_scr[...] = jnp.zeros(acc1_scr.shape, F32)

            def kv_step(j, masked):
                k_rows = blk(j, tq1)
                v = v_ref[k_rows, :]
                for sub in range(2):
                    s = _scores(qaug_scr[sub, rows, :], kaug_scr[sub, k_rows, :])
                    if masked:
                        s = jnp.where(causal, s, -jnp.inf)
                    m_prev = m_scr[sub]
                    m_new = jnp.maximum(m_prev, jnp.max(s, axis=1, keepdims=True))
                    alpha = jnp.exp2(m_prev - m_new)
                    p = jnp.exp2(s - m_new)
                    l_scr[sub] = alpha * l_scr[sub] + jnp.sum(p, axis=1, keepdims=True)
                    acc1_scr[sub] = alpha * acc1_scr[sub] + jnp.dot(
                        p.astype(BF16), v, preferred_element_type=F32)
                    m_scr[sub] = m_new

            def off_diag(j, c):
                kv_step(j, False)
                return c

            lax.fori_loop(0, i, off_diag, 0)
            kv_step(i, True)
            finish(rows, acc1_scr[0], l_scr[0], acc1_scr[1], l_scr[1])
            return carry

        lax.fori_loop(0, s_len // tq1, q_block, 0)


def _l0_attn(qk, vz, lam_vecs, head_g, score_bound, n_heads):
    b, s, _ = qk.shape
    tq = _pick(s, ATTN_TQ)
    tq1 = _pick(s, ATTN_TQ_ONLINE)
    slopes = 2.0 ** (-8.0 * np.arange(1, n_heads + 1) / n_heads)
    slope_tab = jnp.asarray(np.broadcast_to(slopes[:, None], (n_heads, LANES)), dtype=F32)

    def head_block(which):
        return pl.BlockSpec((None, s, HEAD_V), lambda hi, bi: (bi, 0, which * n_heads + hi))

    smem = pl.BlockSpec(memory_space=pltpu.SMEM)
    return pl.pallas_call(
        functools.partial(_attn_kernel, tq=tq, tq1=tq1),
        out_shape=jax.ShapeDtypeStruct((b, s, n_heads * HEAD_V), BF16),
        grid=(n_heads, b),
        in_specs=[smem, head_block(0), head_block(1), head_block(0), head_block(1),
                  _const_spec((4, SUB_HEAD)), _const_spec((1, HEAD_V)),
                  _const_spec((n_heads, LANES))],
        out_specs=pl.BlockSpec((None, s, HEAD_V), lambda hi, bi: (bi, 0, hi)),
        scratch_shapes=[pltpu.VMEM((2, s, HEAD_V), BF16),
                        pltpu.VMEM((2, s, HEAD_V), BF16),
                        pltpu.VMEM((2, s, HEAD_V), BF16),
                        pltpu.VMEM((2, s, HEAD_V), BF16),
                        pltpu.VMEM((s, 2 * HEAD_V), BF16),
                        pltpu.VMEM((2, s, 2 * HEAD_V), F32),
                        pltpu.VMEM((2, tq1, HEAD_V), F32),
                        pltpu.VMEM((2, tq1, 1), F32),
                        pltpu.VMEM((2, tq1, 1), F32)],
        compiler_params=_params("parallel", "arbitrary"),
        name="l0_attn",
    )(score_bound, qk, qk, vz, vz, lam_vecs, head_g.reshape(1, HEAD_V), slope_tab)


def _l0_out_l1_in_kernel(o_ref, x_ref, wo_ref, g_ref, wi_ref, x1_ref, u_ref, z_ref):
    e = u_ref.shape[1]
    x1 = x_ref[...] + jnp.dot(o_ref[...], wo_ref[...], preferred_element_type=F32)
    x1_ref[...] = x1
    ms = jnp.mean(x1 * x1, axis=-1, keepdims=True)
    hn = (x1 * lax.rsqrt(ms + EPS) * g_ref[...]).astype(BF16)
    u_ref[...] = jnp.dot(hn, wi_ref[:, :e], preferred_element_type=F32).astype(BF16)
    z_ref[...] = jnp.dot(hn, wi_ref[:, e:], preferred_element_type=F32).astype(BF16)


def _l0_out_l1_in(o, x, wo_bf, norm_g, wi_bf):
    b, s, d = x.shape
    e = o.shape[2]
    tm = _pick(s, 512)
    act_shape = jax.ShapeDtypeStruct((b, s, e), BF16)
    return pl.pallas_call(
        _l0_out_l1_in_kernel,
        out_shape=(jax.ShapeDtypeStruct((b, s, d), F32), act_shape, act_shape),
        grid=(b, s // tm),
        in_specs=[
            pl.BlockSpec((None, tm, e), lambda bi, si: (bi, si, 0)),
            pl.BlockSpec((None, tm, d), lambda bi, si: (bi, si, 0)),
            _const_spec((e, d)),
            _const_spec((1, d)),
            _const_spec((d, 2 * e)),
        ],
        out_specs=(
            pl.BlockSpec((None, tm, d), lambda bi, si: (bi, si, 0)),
            pl.BlockSpec((None, tm, e), lambda bi, si: (bi, si, 0)),
            pl.BlockSpec((None, tm, e), lambda bi, si: (bi, si, 0)),
        ),
        compiler_params=_params("parallel", "parallel"),
        name="l0_out_l1_in",
    )(o, x, wo_bf, norm_g.reshape(1, d), wi_bf)


def _ssm_kernel(u_ref, bt_ref, a_ref, ct_ref, y_ref, il_in_scr, il_out_scr, st0_scr, st1_scr, st2_scr, h_scr,
                *, n_chunks):
    batch, n_steps, _ = u_ref.shape
    n_slabs = il_in_scr.shape[0]
    half = batch * n_steps // 2
    f = pl.program_id(0)
    st_scrs = (st0_scr, st1_scr, st2_scr)

    @pl.when(f == 0)
    def _():
        for st in st_scrs:
            st[...] = jnp.zeros(st.shape, F32)
        h_scr[...] = jnp.zeros(h_scr.shape, F32)

    keep = jnp.where((f - 1) % n_chunks == 0, 0.0, 1.0).astype(F32)

    def stages(st_b, st_s, st_c):
        for b in range(batch):
            ub = u_ref[b].astype(F32)
            for k in range(n_slabs):
                il_in_scr[k, pl.ds(b, n_steps, stride=batch), :] = ub[:, k * LANES:(k + 1) * LANES]
        for r in range(2):
            rs = slice(r * half, (r + 1) * half)
            u_il = jnp.concatenate([il_in_scr[k, rs, :] for k in range(n_slabs)], axis=1).astype(BF16)
            st_b[rs, :] = jnp.dot(u_il, bt_ref[...], preferred_element_type=F32)

        for w in range(SSM_TILE_STATE // SCAN_LANES):
            re = pl.ds(w * SCAN_LANES, SCAN_LANES)
            im = pl.ds(SSM_TILE_STATE + w * SCAN_LANES, SCAN_LANES)
            a_re = jnp.broadcast_to(a_ref[0:1, re], (batch, SCAN_LANES))
            a_im = jnp.broadcast_to(a_ref[1:2, re], (batch, SCAN_LANES))
            h_re, h_im = h_scr[:, re] * keep, h_scr[:, im] * keep
            for t in range(n_steps):
                rows = pl.ds(t * batch, batch)
                h_re, h_im = (a_re * h_re - a_im * h_im + st_s[rows, re],
                              a_re * h_im + a_im * h_re + st_s[rows, im])
                st_s[rows, re] = h_re
                st_s[rows, im] = h_im
            h_scr[:, re] = h_re
            h_scr[:, im] = h_im

        for r in range(2):
            rs = slice(r * half, (r + 1) * half)
            y = jnp.dot(st_c[rs, :].astype(BF16), ct_ref[...], preferred_element_type=F32)
            for k in range(n_slabs):
                il_out_scr[k, rs, :] = y[:, k * LANES:(k + 1) * LANES]
        for b in range(batch):
            for k in range(n_slabs):
                y_ref[b, :, k * LANES:(k + 1) * LANES] = il_out_scr[
                    k, pl.ds(b, n_steps, stride=batch), :].astype(BF16)

    for res in range(SSM_STAGES):
        pl.when(f % SSM_STAGES == res)(functools.partial(
            stages, st_scrs[res], st_scrs[(res + 2) % SSM_STAGES], st_scrs[(res + 1) % SSM_STAGES]))


def _l1_ssm(u, bt, ct, a_tiles):
    batch, s, e = u.shape
    n_tiles = e // SSM_TILE_CH
    steps = _pick(s, 128)
    rows = steps * batch
    n_chunks = s // steps
    n_items = n_tiles * n_chunks
    blk_shape = (batch, steps, SSM_TILE_CH)
    il_scratch = pltpu.VMEM((SSM_TILE_CH // LANES, rows, LANES), F32)
    st_scratch = pltpu.VMEM((rows, 2 * SSM_TILE_STATE), F32)

    def item(f, lag):
        return jnp.clip(f - lag, 0, n_items - 1)

    def act_map(lag):
        return lambda f: (0, item(f, lag) % n_chunks, item(f, lag) // n_chunks)

    def tile_map(lag):
        return lambda f: (item(f, lag) // n_chunks, 0, 0)

    return pl.pallas_call(
        functools.partial(_ssm_kernel, n_chunks=n_chunks),
        out_shape=jax.ShapeDtypeStruct((batch, s, e), BF16),
        grid=(n_items + SSM_STAGES - 1,),
        in_specs=[
            pl.BlockSpec(blk_shape, act_map(0)),
            pl.BlockSpec((None, SSM_TILE_CH, 2 * SSM_TILE_STATE), tile_map(0)),
            pl.BlockSpec((None, 2, SSM_TILE_STATE), tile_map(1)),
            pl.BlockSpec((None, 2 * SSM_TILE_STATE, SSM_TILE_CH), tile_map(2)),
        ],
        out_specs=pl.BlockSpec(blk_shape, act_map(SSM_STAGES - 1)),
        scratch_shapes=[il_scratch, il_scratch, st_scratch, st_scratch, st_scratch,
                        pltpu.VMEM((batch, 2 * SSM_TILE_STATE), F32)],
        compiler_params=_params("arbitrary"),
        name="l1_ssm",
    )(u, bt, a_tiles, ct)


def _glu_out_kernel(y_ref, u_ref, z_ref, x1_ref, d_ref, wg_ref, bg_ref, wo_ref, o_ref):
    rows = y_ref.shape[0] // GLU_ROW_SPLITS
    for r in range(GLU_ROW_SPLITS):
        rs = slice(r * rows, (r + 1) * rows)
        y = y_ref[rs, :].astype(F32) + d_ref[...] * u_ref[rs, :].astype(F32)
        cdf = 0.5 * (1.0 + jnp.tanh(math.sqrt(2.0 / math.pi) * (y + 0.044715 * (y * y * y))))
        y = y * cdf
        gate = jnp.dot(y.astype(BF16), wg_ref[...], preferred_element_type=F32) + bg_ref[...]
        y = y * (1.0 / (1.0 + jnp.exp(-gate)))
        z = z_ref[rs, :].astype(F32)
        y = y * (z / (1.0 + jnp.exp(-z)))
        o_ref[rs, :] = x1_ref[rs, :] + jnp.dot(y.astype(BF16), wo_ref[...], preferred_element_type=F32)


def _l1_glu_out(y, u, z, x1, d_skip, wg_bf, b_glu, wo_bf):
    b, s, d = x1.shape
    e = wg_bf.shape[0]
    ts = _pick(s, GLU_ROWS)
    act_spec = pl.BlockSpec((None, ts, e), lambda bi, si: (bi, si, 0))
    return pl.pallas_call(
        _glu_out_kernel,
        out_shape=jax.ShapeDtypeStruct((b, s, d), F32),
        grid=(b, s // ts),
        in_specs=[act_spec, act_spec, act_spec,
                  pl.BlockSpec((None, ts, d), lambda bi, si: (bi, si, 0)),
                  _const_spec((1, e)), _const_spec((e, e)), _const_spec((1, e)),
                  _const_spec((e, d))],
        out_specs=pl.BlockSpec((None, ts, d), lambda bi, si: (bi, si, 0)),
        compiler_params=_params("parallel", "parallel"),
        name="l1_glu_out",
    )(y, u, z, x1, d_skip.reshape(1, e), wg_bf, b_glu.reshape(1, e), wo_bf)


def _s5_tiles(lam_re, lam_im, log_dt, b_re, b_im, c_re, c_im):
    g, p = lam_re.shape
    ch = b_re.shape[2]
    nt = g // SSM_TILE_GROUPS
    dt = jnp.exp(log_dt.astype(F32))[:, None]
    lr, li = lam_re.astype(F32), lam_im.astype(F32)
    mag = jnp.exp(lr * dt)
    ab_re, ab_im = mag * jnp.cos(li * dt), mag * jnp.sin(li * dt)
    den = lr * lr + li * li
    nr, ni = ab_re - 1.0, ab_im
    g_re = (nr * lr + ni * li) / den
    g_im = (ni * lr - nr * li) / den
    br, bi = b_re.astype(F32), b_im.astype(F32)
    bb_re = g_re[..., None] * br - g_im[..., None] * bi
    bb_im = g_re[..., None] * bi + g_im[..., None] * br
    eye = jnp.eye(SSM_TILE_GROUPS, dtype=F32)

    def b_tile(bb):
        t = bb.reshape(nt, SSM_TILE_GROUPS, p, ch)
        t = jnp.einsum("ngpc,gh->ngchp", t, eye)
        return t.reshape(nt, SSM_TILE_GROUPS * ch, SSM_TILE_GROUPS * p)

    def c_tile(cc):
        t = cc.reshape(nt, SSM_TILE_GROUPS, ch, p)
        t = jnp.einsum("ngcp,gh->ngphc", t, eye)
        return t.reshape(nt, SSM_TILE_GROUPS * p, SSM_TILE_GROUPS * ch)

    bt = jnp.concatenate([b_tile(bb_re), b_tile(bb_im)], axis=2).astype(BF16)
    ct = jnp.concatenate([c_tile(c_re.astype(F32)), -c_tile(c_im.astype(F32))], axis=1).astype(BF16)
    a_tiles = jnp.stack([ab_re.reshape(nt, SSM_TILE_STATE), ab_im.reshape(nt, SSM_TILE_STATE)], axis=1)
    return bt, ct, a_tiles


def kernel(x, l0_norm_g, l0_w_in, l0_q_norm_g, l0_k_norm_g, l0_lam_q1, l0_lam_k1, l0_lam_q2, l0_lam_k2, l0_head_norm_g, l0_w_out, l1_norm_g, l1_w_in, l1_lam_re, l1_lam_im, l1_log_dt, l1_b_re, l1_b_im, l1_c_re, l1_c_im, l1_d, l1_w_glu, l1_b_glu, l1_w_out):
    b, s, d = x.shape
    e = l0_w_out.shape[0]
    n_heads = e // HEAD_V
    assert s % LANES == 0 and e % MXU_DIM == 0 and b == SUBLANES

    gq = l0_q_norm_g.astype(F32) * (SUB_HEAD ** -0.5 * LOG2E)
    gk = l0_k_norm_g.astype(F32)
    reps = e // SUB_HEAD
    col_scale = jnp.concatenate([jnp.tile(gq, reps), jnp.tile(gk, reps)]).reshape(1, 2 * e)
    qk, vz = _l0_inproj(x.reshape(b * s, d), l0_norm_g.astype(F32), l0_w_in.astype(BF16), col_scale)

    score_bound = (BF16_ROUNDING_SLACK * SUB_HEAD * jnp.max(jnp.abs(gq)) * jnp.max(jnp.abs(gk))).reshape(1)
    lam_vecs = jnp.stack([l0_lam_q1, l0_lam_k1, l0_lam_q2, l0_lam_k2]).astype(F32)
    o = _l0_attn(qk.reshape(b, s, 2 * e), vz.reshape(b, s, 2 * e), lam_vecs,
                 l0_head_norm_g.astype(F32), score_bound, n_heads)

    x1, u, z = _l0_out_l1_in(o, x, l0_w_out.astype(BF16), l1_norm_g.astype(F32),
                             l1_w_in.astype(BF16))

    bt, ct, a_tiles = _s5_tiles(l1_lam_re, l1_lam_im, l1_log_dt, l1_b_re, l1_b_im, l1_c_re, l1_c_im)
    y = _l1_ssm(u, bt, ct, a_tiles)

    return _l1_glu_out(y, u, z, x1, l1_d.astype(F32), l1_w_glu.astype(BF16),
                       l1_b_glu.astype(F32), l1_w_out.astype(BF16))
```

```python
import functools
import math

import jax
import jax.numpy as jnp
import numpy as np
from jax import lax
from jax.experimental import pallas as pl
from jax.experimental.pallas import tpu as pltpu

F32 = jnp.float32
BF16 = jnp.bfloat16

EPS = 1e-6
SUB_HEAD = 64
HEAD_V = 2 * SUB_HEAD
GROUP_CH = 16
GROUP_STATE = 64
LAMBDA_INIT_L0 = 0.8 - 0.6 * math.exp(-0.3 * 0)
LOG2E = math.log2(math.e)

LANES = 128
SUBLANES = 8
MXU_DIM = 256
VMEM_LIMIT_BYTES = 56 * 1024 * 1024
NEG_BIG = -1e30
F32_MIN_EXP = 126

INPROJ_TN = 1024
ATTN_TQ = 512
ATTN_TQ_ONLINE = 256
FIXED_SHIFT_MAX_LOG2 = (F32_MIN_EXP - 26) / 2.0
BF16_ROUNDING_SLACK = 1.01

SSM_TILE_CH = MXU_DIM
SSM_TILE_GROUPS = SSM_TILE_CH // GROUP_CH
SSM_TILE_STATE = SSM_TILE_GROUPS * GROUP_STATE
SCAN_LANES = 256
GLU_ROWS = 512
GLU_ROW_SPLITS = 2
SSM_STAGES = 3


def _params(*sem):
    return pltpu.CompilerParams(dimension_semantics=sem, vmem_limit_bytes=VMEM_LIMIT_BYTES)


def _const_spec(shape):
    nd = len(shape)
    return pl.BlockSpec(shape, lambda *_: (0,) * nd, pipeline_mode=pl.Buffered(1))


def _pick(n, pref):
    t = min(n, pref)
    while n % t:
        t //= 2
    return t


def _l0_inproj_kernel(x_ref, g_ref, wqk_ref, wvz_ref, cs_ref, e_ref, qk_ref, vz_ref, h_scr):
    @pl.when(pl.program_id(1) == 0)
    def _():
        xf = x_ref[...]
        ms = jnp.mean(xf * xf, axis=-1, keepdims=True)
        h_scr[...] = (xf * lax.rsqrt(ms + EPS) * g_ref[...]).astype(BF16)

    hn = h_scr[...]
    acc = jnp.dot(hn, wqk_ref[...], preferred_element_type=F32)
    sq = acc * acc
    hi = sq.astype(BF16)
    lo = (sq - hi.astype(F32)).astype(BF16)
    e = e_ref[...]
    for c in range(acc.shape[1] // MXU_DIM):
        sl = slice(c * MXU_DIM, (c + 1) * MXU_DIM)
        ss = (jnp.dot(hi[:, sl], e, preferred_element_type=F32)
              + jnp.dot(lo[:, sl], e, preferred_element_type=F32))
        inv = lax.rsqrt(ss * (1.0 / SUB_HEAD) + EPS)
        qk_ref[:, sl] = (acc[:, sl] * inv * cs_ref[:, sl]).astype(BF16)
    vz_ref[...] = jnp.dot(hn, wvz_ref[...], preferred_element_type=F32).astype(BF16)


def _l0_inproj(x2d, norm_g, w_bf, col_scale):
    m, d = x2d.shape
    n_half = w_bf.shape[1] // 2
    tm = _pick(m, 1024)
    tn = _pick(n_half, INPROJ_TN)
    n_col_tiles = n_half // tn
    seg = np.arange(MXU_DIM) // SUB_HEAD
    e = jnp.asarray(seg[:, None] == seg[None, :], dtype=BF16)
    out = jax.ShapeDtypeStruct((m, n_half), BF16)
    return pl.pallas_call(
        _l0_inproj_kernel,
        out_shape=(out, out),
        grid=(m // tm, n_col_tiles),
        in_specs=[
            pl.BlockSpec((tm, d), lambda i, j: (i, 0)),
            _const_spec((1, d)),
            pl.BlockSpec((d, tn), lambda i, j: (0, j)),
            pl.BlockSpec((d, tn), lambda i, j: (0, j + n_col_tiles)),
            pl.BlockSpec((1, tn), lambda i, j: (0, j)),
            _const_spec((MXU_DIM, MXU_DIM)),
        ],
        out_specs=(pl.BlockSpec((tm, tn), lambda i, j: (i, j)),
                   pl.BlockSpec((tm, tn), lambda i, j: (i, j))),
        scratch_shapes=[pltpu.VMEM((tm, d), BF16)],
        compiler_params=_params("parallel", "arbitrary"),
        name="l0_inproj",
    )(x2d, norm_g.reshape(1, d), w_bf, w_bf, col_scale, e)


def _split3(val):
    p1 = val.astype(BF16).astype(F32)
    r1 = val - p1
    p2 = r1.astype(BF16).astype(F32)
    p3 = (r1 - p2).astype(BF16).astype(F32)
    return p1, p2, p3


def _augment(parts, lane, off, ones_first):
    p_off = off + 3 if ones_first else off
    o_off = off if ones_first else off + 3
    a = jnp.where(lane == p_off, parts[0], 0.0)
    a = jnp.where(lane == p_off + 1, parts[1], a)
    a = jnp.where(lane == p_off + 2, parts[2], a)
    for t in range(3):
        a = jnp.where(lane == o_off + t, 1.0, a)
    return a


def _scores(qa, kk):
    return lax.dot_general(qa, kk, (((1,), (1,)), ((), ())), preferred_element_type=F32)


def _attn_kernel(shift_ref, q_ref, k_ref, v_ref, z_ref, lam_ref, hg_ref, slope_ref, o_ref,
                 aq_scr, ak_scr, qaug_scr, kaug_scr, vaug_scr, acc_scr, acc1_scr, m_scr, l_scr,
                 *, tq, tq1):
    s_len = q_ref.shape[0]
    n_blk = s_len // tq
    half = tq // 2
    h = pl.program_id(0)
    shift = shift_ref[0]
    slope = slope_ref[pl.ds(h, 1), :] * LOG2E
    lane = lax.broadcasted_iota(jnp.int32, (tq, LANES), 1)
    row = lax.broadcasted_iota(jnp.int32, (tq, LANES), 0)

    lam_v = lam_ref[...]
    lam = (jnp.exp(jnp.sum(lam_v[0:1] * lam_v[1:2], axis=1, keepdims=True))
           - jnp.exp(jnp.sum(lam_v[2:3] * lam_v[3:4], axis=1, keepdims=True)) + LAMBDA_INIT_L0)

    def blk(i, size):
        return pl.ds(pl.multiple_of(i * size, size), size)

    def finish(rows, o0, l0, o1, l1):
        o = o0 * (1.0 / l0) - lam * (o1 * (1.0 / l1))
        ms = jnp.mean(o * o, axis=-1, keepdims=True)
        on = o * lax.rsqrt(ms + EPS) * hg_ref[...] * (1.0 - LAMBDA_INIT_L0)
        z = z_ref[rows, :].astype(F32)
        o_ref[rows, :] = (on * (z / (1.0 + jnp.exp(-z)))).astype(BF16)

    @pl.when(pl.program_id(1) == 0)
    def _tables():
        ones_col = jnp.ones((tq, HEAD_V), BF16)

        def body(c, carry):
            rows = blk(c, tq)
            pos = slope * (c * tq + row).astype(F32)
            k_parts = _split3(pos)
            q_parts = _split3(-(pos + shift))
            ak_scr[0, rows, :] = _augment(k_parts, lane, SUB_HEAD, True).astype(BF16)
            ak_scr[1, rows, :] = _augment(k_parts, lane, 0, True).astype(BF16)
            aq_scr[0, rows, :] = _augment(q_parts, lane, SUB_HEAD, False).astype(BF16)
            aq_scr[1, rows, :] = _augment(q_parts, lane, 0, False).astype(BF16)
            vaug_scr[rows, HEAD_V:2 * HEAD_V] = ones_col
            return carry

        lax.fori_loop(0, n_blk, body, 0)

    lane_row = lax.broadcasted_iota(jnp.int32, (1, LANES), 1)
    keep = (jnp.where(lane_row < SUB_HEAD, 1.0, 0.0).astype(BF16),
            jnp.where(lane_row < SUB_HEAD, 0.0, 1.0).astype(BF16))

    def build(rows):
        kb = k_ref[rows, :]
        qb = q_ref[rows, :]
        for sub in range(2):
            kaug_scr[sub, rows, :] = kb * keep[sub] + ak_scr[sub, rows, :]
            qaug_scr[sub, rows, :] = qb * keep[sub] + aq_scr[sub, rows, :]
        vaug_scr[rows, 0:HEAD_V] = v_ref[rows, :]

    fixed_shift_ok = shift <= FIXED_SHIFT_MAX_LOG2

    @pl.when(fixed_shift_ok)
    def _fixed_shift():
        def tile(q_rows, k_rows, mask_off, init):
            n_q, n_k = q_rows.size, k_rows.size
            for sub in range(2):
                s = _scores(qaug_scr[sub, q_rows, :], kaug_scr[sub, k_rows, :])
                if mask_off is not None:
                    cols = lax.broadcasted_iota(jnp.int32, (n_q, n_k), 1)
                    rws = lax.broadcasted_iota(jnp.int32, (n_q, n_k), 0)
                    s = jnp.where(cols <= rws + mask_off, s, -jnp.inf)
                pv = jnp.dot(jnp.exp2(s).astype(BF16), vaug_scr[k_rows, :], preferred_element_type=F32)
                if init:
                    acc_scr[sub, q_rows, :] = pv
                else:
                    acc_scr[sub, q_rows, :] += pv

        for i in range(n_blk):
            r0 = i * tq
            build(pl.ds(r0, tq))
            tile(pl.ds(r0, half), pl.ds(r0, half), 0, True)
            tile(pl.ds(r0 + half, half), pl.ds(r0, tq), half, True)
            for j in range(i):
                tile(pl.ds(r0, tq), pl.ds(j * tq, tq), None, False)
            rows = pl.ds(r0, tq)
            a0 = acc_scr[0, rows, :]
            a1 = acc_scr[1, rows, :]
            finish(rows, a0[:, :HEAD_V], a0[:, HEAD_V:], a1[:, :HEAD_V], a1[:, HEAD_V:])

    @pl.when(jnp.logical_not(fixed_shift_ok))
    def _online():
        col_s = lax.broadcasted_iota(jnp.int32, (tq1, tq1), 1)
        row_s = lax.broadcasted_iota(jnp.int32, (tq1, tq1), 0)
        causal = col_s <= row_s

        def build_block(c, carry):
            build(blk(c, tq))
            return carry

        lax.fori_loop(0, n_blk, build_block, 0)

        def q_block(i, carry):
            rows = blk(i, tq1)
            m_scr[...] = jnp.full(m_scr.shape, NEG_BIG, F32)
            l_scr[...] = jnp.zeros(l_scr.shape, F32)
            acc1_scr[...] = jnp.zeros(acc1_scr.shape, F32)

            def kv_step(j, masked):
                k_rows = blk(j, tq1)
                v = v_ref[k_rows, :]
                for sub in range(2):
                    s = _scores(qaug_scr[sub, rows, :], kaug_scr[sub, k_rows, :])
                    if masked:
                        s = jnp.where(causal, s, -jnp.inf)
                    m_prev = m_scr[sub]
                    m_new = jnp.maximum(m_prev, jnp.max(s, axis=1, keepdims=True))
                    alpha = jnp.exp2(m_prev - m_new)
                    p = jnp.exp2(s - m_new)
                    l_scr[sub] = alpha * l_scr[sub] + jnp.sum(p, axis=1, keepdims=True)
                    acc1_scr[sub] = alpha * acc1_scr[sub] + jnp.dot(
                        p.astype(BF16), v, preferred_element_type=F32)
                    m_scr[sub] = m_new

            def off_diag(j, c):
                kv_step(j, False)
                return c

            lax.fori_loop(0, i, off_diag, 0)
            kv_step(i, True)
            finish(rows, acc1_scr[0], l_scr[0], acc1_scr[1], l_scr[1])
            return carry

        lax.fori_loop(0, s_len // tq1, q_block, 0)


def _l0_attn(qk, vz, lam_vecs, head_g, score_bound, n_heads):
    b, s, _ = qk.shape
    tq = _pick(s, ATTN_TQ)
    tq1 = _pick(s, ATTN_TQ_ONLINE)
    slopes = 2.0 ** (-8.0 * np.arange(1, n_heads + 1) / n_heads)
    slope_tab = jnp.asarray(np.broadcast_to(slopes[:, None], (n_heads, LANES)), dtype=F32)

    def head_block(which):
        return pl.BlockSpec((None, s, HEAD_V), lambda hi, bi: (bi, 0, which * n_heads + hi))

    smem = pl.BlockSpec(memory_space=pltpu.SMEM)
    return pl.pallas_call(
        functools.partial(_attn_kernel, tq=tq, tq1=tq1),
        out_shape=jax.ShapeDtypeStruct((b, s, n_heads * HEAD_V), BF16),
        grid=(n_heads, b),
        in_specs=[smem, head_block(0), head_block(1), head_block(0), head_block(1),
                  _const_spec((4, SUB_HEAD)), _const_spec((1, HEAD_V)),
                  _const_spec((n_heads, LANES))],
        out_specs=pl.BlockSpec((None, s, HEAD_V), lambda hi, bi: (bi, 0, hi)),
        scratch_shapes=[pltpu.VMEM((2, s, HEAD_V), BF16),
                        pltpu.VMEM((2, s, HEAD_V), BF16),
                        pltpu.VMEM((2, s, HEAD_V), BF16),
                        pltpu.VMEM((2, s, HEAD_V), BF16),
                        pltpu.VMEM((s, 2 * HEAD_V), BF16),
                        pltpu.VMEM((2, s, 2 * HEAD_V), F32),
                        pltpu.VMEM((2, tq1, HEAD_V), F32),
                        pltpu.VMEM((2, tq1, 1), F32),
                        pltpu.VMEM((2, tq1, 1), F32)],
        compiler_params=_params("parallel", "arbitrary"),
        name="l0_attn",
    )(score_bound, qk, qk, vz, vz, lam_vecs, head_g.reshape(1, HEAD_V), slope_tab)


def _l0_out_l1_in_kernel(o_ref, x_ref, wo_ref, g_ref, wi_ref, x1_ref, u_ref, z_ref):
    e = u_ref.shape[1]
    x1 = x_ref[...] + jnp.dot(o_ref[...], wo_ref[...], preferred_element_type=F32)
    x1_ref[...] = x1
    ms = jnp.mean(x1 * x1, axis=-1, keepdims=True)
    hn = (x1 * lax.rsqrt(ms + EPS) * g_ref[...]).astype(BF16)
    u_ref[...] = jnp.dot(hn, wi_ref[:, :e], preferred_element_type=F32).astype(BF16)
    z_ref[...] = jnp.dot(hn, wi_ref[:, e:], preferred_element_type=F32).astype(BF16)


def _l0_out_l1_in(o, x, wo_bf, norm_g, wi_bf):
    b, s, d = x.shape
    e = o.shape[2]
    tm = _pick(s, 512)
    act_shape = jax.ShapeDtypeStruct((b, s, e), BF16)
    return pl.pallas_call(
        _l0_out_l1_in_kernel,
        out_shape=(jax.ShapeDtypeStruct((b, s, d), F32), act_shape, act_shape),
        grid=(b, s // tm),
        in_specs=[
            pl.BlockSpec((None, tm, e), lambda bi, si: (bi, si, 0)),
            pl.BlockSpec((None, tm, d), lambda bi, si: (bi, si, 0)),
            _const_spec((e, d)),
            _const_spec((1, d)),
            _const_spec((d, 2 * e)),
        ],
        out_specs=(
            pl.BlockSpec((None, tm, d), lambda bi, si: (bi, si, 0)),
            pl.BlockSpec((None, tm, e), lambda bi, si: (bi, si, 0)),
            pl.BlockSpec((None, tm, e), lambda bi, si: (bi, si, 0)),
        ),
        compiler_params=_params("parallel", "parallel"),
        name="l0_out_l1_in",
    )(o, x, wo_bf, norm_g.reshape(1, d), wi_bf)


def _ssm_kernel(u_ref, bt_ref, a_ref, ct_ref, y_ref, il_in_scr, il_out_scr, st0_scr, st1_scr, st2_scr, h_scr,
                *, n_chunks):
    batch, n_steps, _ = u_ref.shape
    n_slabs = il_in_scr.shape[0]
    half = batch * n_steps // 2
    f = pl.program_id(0)
    st_scrs = (st0_scr, st1_scr, st2_scr)

    @pl.when(f == 0)
    def _():
        for st in st_scrs:
            st[...] = jnp.zeros(st.shape, F32)
        h_scr[...] = jnp.zeros(h_scr.shape, F32)

    keep = jnp.where((f - 1) % n_chunks == 0, 0.0, 1.0).astype(F32)

    def stages(st_b, st_s, st_c):
        for b in range(batch):
            ub = u_ref[b].astype(F32)
            for k in range(n_slabs):
                il_in_scr[k, pl.ds(b, n_steps, stride=batch), :] = ub[:, k * LANES:(k + 1) * LANES]
        for r in range(2):
            rs = slice(r * half, (r + 1) * half)
            u_il = jnp.concatenate([il_in_scr[k, rs, :] for k in range(n_slabs)], axis=1).astype(BF16)
            st_b[rs, :] = jnp.dot(u_il, bt_ref[...], preferred_element_type=F32)

        for w in range(SSM_TILE_STATE // SCAN_LANES):
            re = pl.ds(w * SCAN_LANES, SCAN_LANES)
            im = pl.ds(SSM_TILE_STATE + w * SCAN_LANES, SCAN_LANES)
            a_re = jnp.broadcast_to(a_ref[0:1, re], (batch, SCAN_LANES))
            a_im = jnp.broadcast_to(a_ref[1:2, re], (batch, SCAN_LANES))
            h_re, h_im = h_scr[:, re] * keep, h_scr[:, im] * keep
            for t in range(n_steps):
                rows = pl.ds(t * batch, batch)
                h_re, h_im = (a_re * h_re - a_im * h_im + st_s[rows, re],
                              a_re * h_im + a_im * h_re + st_s[rows, im])
                st_s[rows, re] = h_re
                st_s[rows, im] = h_im
            h_scr[:, re] = h_re
            h_scr[:, im] = h_im

        for r in range(2):
            rs = slice(r * half, (r + 1) * half)
            y = jnp.dot(st_c[rs, :].astype(BF16), ct_ref[...], preferred_element_type=F32)
            for k in range(n_slabs):
                il_out_scr[k, rs, :] = y[:, k * LANES:(k + 1) * LANES]
        for b in range(batch):
            for k in range(n_slabs):
                y_ref[b, :, k * LANES:(k + 1) * LANES] = il_out_scr[
                    k, pl.ds(b, n_steps, stride=batch), :].astype(BF16)

    for res in range(SSM_STAGES):
        pl.when(f % SSM_STAGES == res)(functools.partial(
            stages, st_scrs[res], st_scrs[(res + 2) % SSM_STAGES], st_scrs[(res + 1) % SSM_STAGES]))


def _l1_ssm(u, bt, ct, a_tiles):
    batch, s, e = u.shape
    n_tiles = e // SSM_TILE_CH
    steps = _pick(s, 128)
    rows = steps * batch
    n_chunks = s // steps
    n_items = n_tiles * n_chunks
    blk_shape = (batch, steps, SSM_TILE_CH)
    il_scratch = pltpu.VMEM((SSM_TILE_CH // LANES, rows, LANES), F32)
    st_scratch = pltpu.VMEM((rows, 2 * SSM_TILE_STATE), F32)

    def item(f, lag):
        return jnp.clip(f - lag, 0, n_items - 1)

    def act_map(lag):
        return lambda f: (0, item(f, lag) % n_chunks, item(f, lag) // n_chunks)

    def tile_map(lag):
        return lambda f: (item(f, lag) // n_chunks, 0, 0)

    return pl.pallas_call(
        functools.partial(_ssm_kernel, n_chunks=n_chunks),
        out_shape=jax.ShapeDtypeStruct((batch, s, e), BF16),
        grid=(n_items + SSM_STAGES - 1,),
        in_specs=[
            pl.BlockSpec(blk_shape, act_map(0)),
            pl.BlockSpec((None, SSM_TILE_CH, 2 * SSM_TILE_STATE), tile_map(0)),
            pl.BlockSpec((None, 2, SSM_TILE_STATE), tile_map(1)),
            pl.BlockSpec((None, 2 * SSM_TILE_STATE, SSM_TILE_CH), tile_map(2)),
        ],
        out_specs=pl.BlockSpec(blk_shape, act_map(SSM_STAGES - 1)),
        scratch_shapes=[il_scratch, il_scratch, st_scratch, st_scratch, st_scratch,
                        pltpu.VMEM((batch, 2 * SSM_TILE_STATE), F32)],
        compiler_params=_params("arbitrary"),
        name="l1_ssm",
    )(u, bt, a_tiles, ct)


def _glu_out_kernel(y_ref, u_ref, z_ref, x1_ref, d_ref, wg_ref, bg_ref, wo_ref, o_ref):
    rows = y_ref.shape[0] // GLU_ROW_SPLITS
    for r in range(GLU_ROW_SPLITS):
        rs = slice(r * rows, (r + 1) * rows)
        y = y_ref[rs, :].astype(F32) + d_ref[...] * u_ref[rs, :].astype(F32)
        cdf = 0.5 * (1.0 + jnp.tanh(math.sqrt(2.0 / math.pi) * (y + 0.044715 * (y * y * y))))
        y = y * cdf
        gate = jnp.dot(y.astype(BF16), wg_ref[...], preferred_element_type=F32) + bg_ref[...]
        y = y * (1.0 / (1.0 + jnp.exp(-gate)))
        z = z_ref[rs, :].astype(F32)
        y = y * (z / (1.0 + jnp.exp(-z)))
        o_ref[rs, :] = x1_ref[rs, :] + jnp.dot(y.astype(BF16), wo_ref[...], preferred_element_type=F32)


def _l1_glu_out(y, u, z, x1, d_skip, wg_bf, b_glu, wo_bf):
    b, s, d = x1.shape
    e = wg_bf.shape[0]
    ts = _pick(s, GLU_ROWS)
    act_spec = pl.BlockSpec((None, ts, e), lambda bi, si: (bi, si, 0))
    return pl.pallas_call(
        _glu_out_kernel,
        out_shape=jax.ShapeDtypeStruct((b, s, d), F32),
        grid=(b, s // ts),
        in_specs=[act_spec, act_spec, act_spec,
                  pl.BlockSpec((None, ts, d), lambda bi, si: (bi, si, 0)),
                  _const_spec((1, e)), _const_spec((e, e)), _const_spec((1, e)),
                  _const_spec((e, d))],
        out_specs=pl.BlockSpec((None, ts, d), lambda bi, si: (bi, si, 0)),
        compiler_params=_params("parallel", "parallel"),
        name="l1_glu_out",
    )(y, u, z, x1, d_skip.reshape(1, e), wg_bf, b_glu.reshape(1, e), wo_bf)


def _s5_tiles(lam_re, lam_im, log_dt, b_re, b_im, c_re, c_im):
    g, p = lam_re.shape
    ch = b_re.shape[2]
    nt = g // SSM_TILE_GROUPS
    dt = jnp.exp(log_dt.astype(F32))[:, None]
    lr, li = lam_re.astype(F32), lam_im.astype(F32)
    mag = jnp.exp(lr * dt)
    ab_re, ab_im = mag * jnp.cos(li * dt), mag * jnp.sin(li * dt)
    den = lr * lr + li * li
    nr, ni = ab_re - 1.0, ab_im
    g_re = (nr * lr + ni * li) / den
    g_im = (ni * lr - nr * li) / den
    br, bi = b_re.astype(F32), b_im.astype(F32)
    bb_re = g_re[..., None] * br - g_im[..., None] * bi
    bb_im = g_re[..., None] * bi + g_im[..., None] * br
    eye = jnp.eye(SSM_TILE_GROUPS, dtype=BF16)

    def b_tile(bb):
        t = bb.astype(BF16).reshape(nt, SSM_TILE_GROUPS, p, ch)
        t = jnp.einsum("ngpc,gh->ngchp", t, eye)
        return t.reshape(nt, SSM_TILE_GROUPS * ch, SSM_TILE_GROUPS * p)

    def c_tile(cc):
        t = cc.astype(BF16).reshape(nt, SSM_TILE_GROUPS, ch, p)
        t = jnp.einsum("ngcp,gh->ngphc", t, eye)
        return t.reshape(nt, SSM_TILE_GROUPS * p, SSM_TILE_GROUPS * ch)

    bt = jnp.concatenate([b_tile(bb_re), b_tile(bb_im)], axis=2)
    ct = jnp.concatenate([c_tile(c_re.astype(F32)), c_tile(-c_im.astype(F32))], axis=1)
    a_tiles = jnp.stack([ab_re.reshape(nt, SSM_TILE_STATE), ab_im.reshape(nt, SSM_TILE_STATE)], axis=1)
    return bt, ct, a_tiles


def kernel(x, l0_norm_g, l0_w_in, l0_q_norm_g, l0_k_norm_g, l0_lam_q1, l0_lam_k1, l0_lam_q2, l0_lam_k2, l0_head_norm_g, l0_w_out, l1_norm_g, l1_w_in, l1_lam_re, l1_lam_im, l1_log_dt, l1_b_re, l1_b_im, l1_c_re, l1_c_im, l1_d, l1_w_glu, l1_b_glu, l1_w_out):
    b, s, d = x.shape
    e = l0_w_out.shape[0]
    n_heads = e // HEAD_V
    assert s % LANES == 0 and e % MXU_DIM == 0 and b == SUBLANES

    gq = l0_q_norm_g.astype(F32) * (SUB_HEAD ** -0.5 * LOG2E)
    gk = l0_k_norm_g.astype(F32)
    reps = e // SUB_HEAD
    col_scale = jnp.concatenate([jnp.tile(gq, reps), jnp.tile(gk, reps)]).reshape(1, 2 * e)
    qk, vz = _l0_inproj(x.reshape(b * s, d), l0_norm_g.astype(F32), l0_w_in.astype(BF16), col_scale)

    score_bound = (BF16_ROUNDING_SLACK * SUB_HEAD * jnp.max(jnp.abs(gq)) * jnp.max(jnp.abs(gk))).reshape(1)
    lam_vecs = jnp.stack([l0_lam_q1, l0_lam_k1, l0_lam_q2, l0_lam_k2]).astype(F32)
    o = _l0_attn(qk.reshape(b, s, 2 * e), vz.reshape(b, s, 2 * e), lam_vecs,
                 l0_head_norm_g.astype(F32), score_bound, n_heads)

    x1, u, z = _l0_out_l1_in(o, x, l0_w_out.astype(BF16), l1_norm_g.astype(F32),
                             l1_w_in.astype(BF16))

    bt, ct, a_tiles = _s5_tiles(l1_lam_re, l1_lam_im, l1_log_dt, l1_b_re, l1_b_im, l1_c_re, l1_c_im)
    y = _l1_ssm(u, bt, ct, a_tiles)

    return _l1_glu_out(y, u, z, x1, l1_d.astype(F32), l1_w_glu.astype(BF16),
                       l1_b_glu.astype(F32), l1_w_out.astype(BF16))
```

```python
import functools
import math

import jax
import jax.numpy as jnp
import numpy as np
from jax import lax
from jax.experimental import pallas as pl
from jax.experimental.pallas import tpu as pltpu

F32 = jnp.float32
BF16 = jnp.bfloat16

EPS = 1e-6
SUB_HEAD = 64
HEAD_V = 2 * SUB_HEAD
GROUP_CH = 16
GROUP_STATE = 64
LAMBDA_INIT_L0 = 0.8 - 0.6 * math.exp(-0.3 * 0)
LOG2E = math.log2(math.e)

LANES = 128
SUBLANES = 8
MXU_DIM = 256
VMEM_LIMIT_BYTES = 56 * 1024 * 1024
NEG_BIG = -1e30
F32_MIN_EXP = 126

INPROJ_TN = 1024
ATTN_TQ = 512
ATTN_TQ_ONLINE = 256
FIXED_SHIFT_MAX_LOG2 = (F32_MIN_EXP - 26) / 2.0
BF16_ROUNDING_SLACK = 1.01

SSM_TILE_CH = MXU_DIM
SSM_TILE_GROUPS = SSM_TILE_CH // GROUP_CH
SSM_TILE_STATE = SSM_TILE_GROUPS * GROUP_STATE
SCAN_LANES = 256
GLU_ROWS = 512
GLU_ROW_SPLITS = 2
SSM_STAGES = 3


def _params(*sem):
    return pltpu.CompilerParams(dimension_semantics=sem, vmem_limit_bytes=VMEM_LIMIT_BYTES)


def _const_spec(shape):
    nd = len(shape)
    return pl.BlockSpec(shape, lambda *_: (0,) * nd, pipeline_mode=pl.Buffered(1))


def _pick(n, pref):
    t = min(n, pref)
    while n % t:
        t //= 2
    return t


def _l0_inproj_kernel(x_ref, g_ref, wqk_ref, wvz_ref, cs_ref, e_ref, qk_ref, vz_ref):
    xf = x_ref[...]
    ms = jnp.mean(xf * xf, axis=-1, keepdims=True)
    hn = (xf * lax.rsqrt(ms + EPS) * g_ref[...]).astype(BF16)
    acc = jnp.dot(hn, wqk_ref[...], preferred_element_type=F32)
    sq = acc * acc
    hi = sq.astype(BF16)
    lo = (sq - hi.astype(F32)).astype(BF16)
    e = e_ref[...]
    for c in range(acc.shape[1] // MXU_DIM):
        sl = slice(c * MXU_DIM, (c + 1) * MXU_DIM)
        ss = (jnp.dot(hi[:, sl], e, preferred_element_type=F32)
              + jnp.dot(lo[:, sl], e, preferred_element_type=F32))
        inv = lax.rsqrt(ss * (1.0 / SUB_HEAD) + EPS)
        qk_ref[:, sl] = (acc[:, sl] * inv * cs_ref[:, sl]).astype(BF16)
    vz_ref[...] = jnp.dot(hn, wvz_ref[...], preferred_element_type=F32).astype(BF16)


def _l0_inproj(x2d, norm_g, w_bf, col_scale):
    m, d = x2d.shape
    n_half = w_bf.shape[1] // 2
    tm = _pick(m, 1024)
    tn = _pick(n_half, INPROJ_TN)
    n_col_tiles = n_half // tn
    seg = np.arange(MXU_DIM) // SUB_HEAD
    e = jnp.asarray(seg[:, None] == seg[None, :], dtype=BF16)
    out = jax.ShapeDtypeStruct((m, n_half), BF16)
    return pl.pallas_call(
        _l0_inproj_kernel,
        out_shape=(out, out),
        grid=(m // tm, n_col_tiles),
        in_specs=[
            pl.BlockSpec((tm, d), lambda i, j: (i, 0)),
            _const_spec((1, d)),
            pl.BlockSpec((d, tn), lambda i, j: (0, j)),
            pl.BlockSpec((d, tn), lambda i, j: (0, j + n_col_tiles)),
            pl.BlockSpec((1, tn), lambda i, j: (0, j)),
            _const_spec((MXU_DIM, MXU_DIM)),
        ],
        out_specs=(pl.BlockSpec((tm, tn), lambda i, j: (i, j)),
                   pl.BlockSpec((tm, tn), lambda i, j: (i, j))),
        compiler_params=_params("parallel", "parallel"),
        name="l0_inproj",
    )(x2d, norm_g.reshape(1, d), w_bf, w_bf, col_scale, e)


def _split3(val):
    p1 = val.astype(BF16).astype(F32)
    r1 = val - p1
    p2 = r1.astype(BF16).astype(F32)
    p3 = (r1 - p2).astype(BF16).astype(F32)
    return p1, p2, p3


def _augment(parts, lane, off, ones_first):
    p_off = off + 3 if ones_first else off
    o_off = off if ones_first else off + 3
    a = jnp.where(lane == p_off, parts[0], 0.0)
    a = jnp.where(lane == p_off + 1, parts[1], a)
    a = jnp.where(lane == p_off + 2, parts[2], a)
    for t in range(3):
        a = jnp.where(lane == o_off + t, 1.0, a)
    return a


def _scores(qa, kk):
    return lax.dot_general(qa, kk, (((1,), (1,)), ((), ())), preferred_element_type=F32)


def _attn_kernel(shift_ref, q_ref, k_ref, v_ref, z_ref, lam_ref, hg_ref, slope_ref, o_ref,
                 aq_scr, ak_scr, qaug_scr, kaug_scr, vaug_scr, acc_scr, acc1_scr, m_scr, l_scr,
                 *, tq, tq1):
    s_len = q_ref.shape[0]
    n_blk = s_len // tq
    half = tq // 2
    h = pl.program_id(0)
    shift = shift_ref[0]
    slope = slope_ref[pl.ds(h, 1), :] * LOG2E
    lane = lax.broadcasted_iota(jnp.int32, (tq, LANES), 1)
    row = lax.broadcasted_iota(jnp.int32, (tq, LANES), 0)

    lam_v = lam_ref[...]
    lam = (jnp.exp(jnp.sum(lam_v[0:1] * lam_v[1:2], axis=1, keepdims=True))
           - jnp.exp(jnp.sum(lam_v[2:3] * lam_v[3:4], axis=1, keepdims=True)) + LAMBDA_INIT_L0)

    def blk(i, size):
        return pl.ds(pl.multiple_of(i * size, size), size)

    def finish(rows, o0, l0, o1, l1):
        o = o0 * (1.0 / l0) - lam * (o1 * (1.0 / l1))
        ms = jnp.mean(o * o, axis=-1, keepdims=True)
        on = o * lax.rsqrt(ms + EPS) * hg_ref[...] * (1.0 - LAMBDA_INIT_L0)
        z = z_ref[rows, :].astype(F32)
        o_ref[rows, :] = (on * (z / (1.0 + jnp.exp(-z)))).astype(BF16)

    @pl.when(pl.program_id(1) == 0)
    def _tables():
        ones_col = jnp.ones((tq, HEAD_V), BF16)

        def body(c, carry):
            rows = blk(c, tq)
            pos = slope * (c * tq + row).astype(F32)
            k_parts = _split3(pos)
            q_parts = _split3(-(pos + shift))
            ak_scr[0, rows, :] = _augment(k_parts, lane, SUB_HEAD, True).astype(BF16)
            ak_scr[1, rows, :] = _augment(k_parts, lane, 0, True).astype(BF16)
            aq_scr[0, rows, :] = _augment(q_parts, lane, SUB_HEAD, False).astype(BF16)
            aq_scr[1, rows, :] = _augment(q_parts, lane, 0, False).astype(BF16)
            vaug_scr[rows, HEAD_V:2 * HEAD_V] = ones_col
            return carry

        lax.fori_loop(0, n_blk, body, 0)

    lane_row = lax.broadcasted_iota(jnp.int32, (1, LANES), 1)
    keep = (jnp.where(lane_row < SUB_HEAD, 1.0, 0.0).astype(BF16),
            jnp.where(lane_row < SUB_HEAD, 0.0, 1.0).astype(BF16))

    def build(rows):
        kb = k_ref[rows, :]
        qb = q_ref[rows, :]
        for sub in range(2):
            kaug_scr[sub, rows, :] = kb * keep[sub] + ak_scr[sub, rows, :]
            qaug_scr[sub, rows, :] = qb * keep[sub] + aq_scr[sub, rows, :]
        vaug_scr[rows, 0:HEAD_V] = v_ref[rows, :]

    fixed_shift_ok = shift <= FIXED_SHIFT_MAX_LOG2

    @pl.when(fixed_shift_ok)
    def _fixed_shift():
        def tile(q_rows, k_rows, mask_off, init):
            n_q, n_k = q_rows.size, k_rows.size
            for sub in range(2):
                s = _scores(qaug_scr[sub, q_rows, :], kaug_scr[sub, k_rows, :])
                if mask_off is not None:
                    cols = lax.broadcasted_iota(jnp.int32, (n_q, n_k), 1)
                    rws = lax.broadcasted_iota(jnp.int32, (n_q, n_k), 0)
                    s = jnp.where(cols <= rws + mask_off, s, -jnp.inf)
                pv = jnp.dot(jnp.exp2(s).astype(BF16), vaug_scr[k_rows, :], preferred_element_type=F32)
                if init:
                    acc_scr[sub, q_rows, :] = pv
                else:
                    acc_scr[sub, q_rows, :] += pv

        for i in range(n_blk):
            r0 = i * tq
            build(pl.ds(r0, tq))
            tile(pl.ds(r0, half), pl.ds(r0, half), 0, True)
            tile(pl.ds(r0 + half, half), pl.ds(r0, tq), half, True)
            for j in range(i):
                tile(pl.ds(r0, tq), pl.ds(j * tq, tq), None, False)
            rows = pl.ds(r0, tq)
            a0 = acc_scr[0, rows, :]
            a1 = acc_scr[1, rows, :]
            finish(rows, a0[:, :HEAD_V], a0[:, HEAD_V:], a1[:, :HEAD_V], a1[:, HEAD_V:])

    @pl.when(jnp.logical_not(fixed_shift_ok))
    def _online():
        col_s = lax.broadcasted_iota(jnp.int32, (tq1, tq1), 1)
        row_s = lax.broadcasted_iota(jnp.int32, (tq1, tq1), 0)
        causal = col_s <= row_s

        def build_block(c, carry):
            build(blk(c, tq))
            return carry

        lax.fori_loop(0, n_blk, build_block, 0)

        def q_block(i, carry):
            rows = blk(i, tq1)
            m_scr[...] = jnp.full(m_scr.shape, NEG_BIG, F32)
            l_scr[...] = jnp.zeros(l_scr.shape, F32)
            acc1_scr[...] = jnp.zeros(acc1_scr.shape, F32)

            def kv_step(j, masked):
                k_rows = blk(j, tq1)
                v = v_ref[k_rows, :]
                for sub in range(2):
                    s = _scores(qaug_scr[sub, rows, :], kaug_scr[sub, k_rows, :])
                    if masked:
                        s = jnp.where(causal, s, -jnp.inf)
                    m_prev = m_scr[sub]
                    m_new = jnp.maximum(m_prev, jnp.max(s, axis=1, keepdims=True))
                    alpha = jnp.exp2(m_prev - m_new)
                    p = jnp.exp2(s - m_new)
                    l_scr[sub] = alpha * l_scr[sub] + jnp.sum(p, axis=1, keepdims=True)
                    acc1_scr[sub] = alpha * acc1_scr[sub] + jnp.dot(
                        p.astype(BF16), v, preferred_element_type=F32)
                    m_scr[sub] = m_new

            def off_diag(j, c):
                kv_step(j, False)
                return c

            lax.fori_loop(0, i, off_diag, 0)
            kv_step(i, True)
            finish(rows, acc1_scr[0], l_scr[0], acc1_scr[1], l_scr[1])
            return carry

        lax.fori_loop(0, s_len // tq1, q_block, 0)


def _l0_attn(qk, vz, lam_vecs, head_g, score_bound, n_heads):
    b, s, _ = qk.shape
    tq = _pick(s, ATTN_TQ)
    tq1 = _pick(s, ATTN_TQ_ONLINE)
    slopes = 2.0 ** (-8.0 * np.arange(1, n_heads + 1) / n_heads)
    slope_tab = jnp.asarray(np.broadcast_to(slopes[:, None], (n_heads, LANES)), dtype=F32)

    def head_block(which):
        return pl.BlockSpec((None, s, HEAD_V), lambda hi, bi: (bi, 0, which * n_heads + hi))

    smem = pl.BlockSpec(memory_space=pltpu.SMEM)
    return pl.pallas_call(
        functools.partial(_attn_kernel, tq=tq, tq1=tq1),
        out_shape=jax.ShapeDtypeStruct((b, s, n_heads * HEAD_V), BF16),
        grid=(n_heads, b),
        in_specs=[smem, head_block(0), head_block(1), head_block(0), head_block(1),
                  _const_spec((4, SUB_HEAD)), _const_spec((1, HEAD_V)),
                  _const_spec((n_heads, LANES))],
        out_specs=pl.BlockSpec((None, s, HEAD_V), lambda hi, bi: (bi, 0, hi)),
        scratch_shapes=[pltpu.VMEM((2, s, HEAD_V), BF16),
                        pltpu.VMEM((2, s, HEAD_V), BF16),
                        pltpu.VMEM((2, s, HEAD_V), BF16),
                        pltpu.VMEM((2, s, HEAD_V), BF16),
                        pltpu.VMEM((s, 2 * HEAD_V), BF16),
                        pltpu.VMEM((2, s, 2 * HEAD_V), F32),
                        pltpu.VMEM((2, tq1, HEAD_V), F32),
                        pltpu.VMEM((2, tq1, 1), F32),
                        pltpu.VMEM((2, tq1, 1), F32)],
        compiler_params=_params("parallel", "arbitrary"),
        name="l0_attn",
    )(score_bound, qk, qk, vz, vz, lam_vecs, head_g.reshape(1, HEAD_V), slope_tab)


def _l0_out_l1_in_kernel(o_ref, x_ref, wo_ref, g_ref, wi_ref, x1_ref, u_ref, z_ref):
    e = u_ref.shape[1]
    x1 = x_ref[...] + jnp.dot(o_ref[...], wo_ref[...], preferred_element_type=F32)
    x1_ref[...] = x1
    ms = jnp.mean(x1 * x1, axis=-1, keepdims=True)
    hn = (x1 * lax.rsqrt(ms + EPS) * g_ref[...]).astype(BF16)
    u_ref[...] = jnp.dot(hn, wi_ref[:, :e], preferred_element_type=F32).astype(BF16)
    z_ref[...] = jnp.dot(hn, wi_ref[:, e:], preferred_element_type=F32).astype(BF16)


def _l0_out_l1_in(o, x, wo_bf, norm_g, wi_bf):
    b, s, d = x.shape
    e = o.shape[2]
    tm = _pick(s, 512)
    act_shape = jax.ShapeDtypeStruct((b, s, e), BF16)
    return pl.pallas_call(
        _l0_out_l1_in_kernel,
        out_shape=(jax.ShapeDtypeStruct((b, s, d), F32), act_shape, act_shape),
        grid=(b, s // tm),
        in_specs=[
            pl.BlockSpec((None, tm, e), lambda bi, si: (bi, si, 0)),
            pl.BlockSpec((None, tm, d), lambda bi, si: (bi, si, 0)),
            _const_spec((e, d)),
            _const_spec((1, d)),
            _const_spec((d, 2 * e)),
        ],
        out_specs=(
            pl.BlockSpec((None, tm, d), lambda bi, si: (bi, si, 0)),
            pl.BlockSpec((None, tm, e), lambda bi, si: (bi, si, 0)),
            pl.BlockSpec((None, tm, e), lambda bi, si: (bi, si, 0)),
        ),
        compiler_params=_params("parallel", "parallel"),
        name="l0_out_l1_in",
    )(o, x, wo_bf, norm_g.reshape(1, d), wi_bf)


def _ssm_kernel(u_ref, bt_ref, a_ref, ct_ref, y_ref, il_in_scr, il_out_scr, st0_scr, st1_scr, st2_scr, h_scr,
                *, n_chunks):
    batch, n_steps, _ = u_ref.shape
    n_slabs = il_in_scr.shape[0]
    half = batch * n_steps // 2
    f = pl.program_id(0)
    st_scrs = (st0_scr, st1_scr, st2_scr)

    @pl.when(f == 0)
    def _():
        for st in st_scrs:
            st[...] = jnp.zeros(st.shape, F32)
        h_scr[...] = jnp.zeros(h_scr.shape, F32)

    keep = jnp.where((f - 1) % n_chunks == 0, 0.0, 1.0).astype(F32)

    def stages(st_b, st_s, st_c):
        for b in range(batch):
            ub = u_ref[b].astype(F32)
            for k in range(n_slabs):
                il_in_scr[k, pl.ds(b, n_steps, stride=batch), :] = ub[:, k * LANES:(k + 1) * LANES]
        for r in range(2):
            rs = slice(r * half, (r + 1) * half)
            u_il = jnp.concatenate([il_in_scr[k, rs, :] for k in range(n_slabs)], axis=1).astype(BF16)
            st_b[rs, :] = jnp.dot(u_il, bt_ref[...], preferred_element_type=F32)

        for w in range(SSM_TILE_STATE // SCAN_LANES):
            re = pl.ds(w * SCAN_LANES, SCAN_LANES)
            im = pl.ds(SSM_TILE_STATE + w * SCAN_LANES, SCAN_LANES)
            a_re = jnp.broadcast_to(a_ref[0:1, re], (batch, SCAN_LANES))
            a_im = jnp.broadcast_to(a_ref[1:2, re], (batch, SCAN_LANES))
            h_re, h_im = h_scr[:, re] * keep, h_scr[:, im] * keep
            for t in range(n_steps):
                rows = pl.ds(t * batch, batch)
                h_re, h_im = (a_re * h_re - a_im * h_im + st_s[rows, re],
                              a_re * h_im + a_im * h_re + st_s[rows, im])
                st_s[rows, re] = h_re
                st_s[rows, im] = h_im
            h_scr[:, re] = h_re
            h_scr[:, im] = h_im

        for r in range(2):
            rs = slice(r * half, (r + 1) * half)
            y = jnp.dot(st_c[rs, :].astype(BF16), ct_ref[...], preferred_element_type=F32)
            for k in range(n_slabs):
                il_out_scr[k, rs, :] = y[:, k * LANES:(k + 1) * LANES]
        for b in range(batch):
            for k in range(n_slabs):
                y_ref[b, :, k * LANES:(k + 1) * LANES] = il_out_scr[
                    k, pl.ds(b, n_steps, stride=batch), :].astype(BF16)

    for res in range(SSM_STAGES):
        pl.when(f % SSM_STAGES == res)(functools.partial(
            stages, st_scrs[res], st_scrs[(res + 2) % SSM_STAGES], st_scrs[(res + 1) % SSM_STAGES]))


def _l1_ssm(u, bt, ct, a_tiles):
    batch, s, e = u.shape
    n_tiles = e // SSM_TILE_CH
    steps = _pick(s, 128)
    rows = steps * batch
    n_chunks = s // steps
    n_items = n_tiles * n_chunks
    blk_shape = (batch, steps, SSM_TILE_CH)
    il_scratch = pltpu.VMEM((SSM_TILE_CH // LANES, rows, LANES), F32)
    st_scratch = pltpu.VMEM((rows, 2 * SSM_TILE_STATE), F32)

    def item(f, lag):
        return jnp.clip(f - lag, 0, n_items - 1)

    def act_map(lag):
        return lambda f: (0, item(f, lag) % n_chunks, item(f, lag) // n_chunks)

    def tile_map(lag):
        return lambda f: (item(f, lag) // n_chunks, 0, 0)

    return pl.pallas_call(
        functools.partial(_ssm_kernel, n_chunks=n_chunks),
        out_shape=jax.ShapeDtypeStruct((batch, s, e), BF16),
        grid=(n_items + SSM_STAGES - 1,),
        in_specs=[
            pl.BlockSpec(blk_shape, act_map(0)),
            pl.BlockSpec((None, SSM_TILE_CH, 2 * SSM_TILE_STATE), tile_map(0)),
            pl.BlockSpec((None, 2, SSM_TILE_STATE), tile_map(1)),
            pl.BlockSpec((None, 2 * SSM_TILE_STATE, SSM_TILE_CH), tile_map(2)),
        ],
        out_specs=pl.BlockSpec(blk_shape, act_map(SSM_STAGES - 1)),
        scratch_shapes=[il_scratch, il_scratch, st_scratch, st_scratch, st_scratch,
                        pltpu.VMEM((batch, 2 * SSM_TILE_STATE), F32)],
        compiler_params=_params("arbitrary"),
        name="l1_ssm",
    )(u, bt, a_tiles, ct)


def _glu_out_kernel(y_ref, u_ref, z_ref, x1_ref, d_ref, wg_ref, bg_ref, wo_ref, o_ref):
    rows = y_ref.shape[0] // GLU_ROW_SPLITS
    for r in range(GLU_ROW_SPLITS):
        rs = slice(r * rows, (r + 1) * rows)
        y = y_ref[rs, :].astype(F32) + d_ref[...] * u_ref[rs, :].astype(F32)
        cdf = 0.5 * (1.0 + jnp.tanh(math.sqrt(2.0 / math.pi) * (y + 0.044715 * (y * y * y))))
        y = y * cdf
        gate = jnp.dot(y.astype(BF16), wg_ref[...], preferred_element_type=F32) + bg_ref[...]
        y = y * (1.0 / (1.0 + jnp.exp(-gate)))
        z = z_ref[rs, :].astype(F32)
        y = y * (z / (1.0 + jnp.exp(-z)))
        o_ref[rs, :] = x1_ref[rs, :] + jnp.dot(y.astype(BF16), wo_ref[...], preferred_element_type=F32)


def _l1_glu_out(y, u, z, x1, d_skip, wg_bf, b_glu, wo_bf):
    b, s, d = x1.shape
    e = wg_bf.shape[0]
    ts = _pick(s, GLU_ROWS)
    act_spec = pl.BlockSpec((None, ts, e), lambda bi, si: (bi, si, 0))
    return pl.pallas_call(
        _glu_out_kernel,
        out_shape=jax.ShapeDtypeStruct((b, s, d), F32),
        grid=(b, s // ts),
        in_specs=[act_spec, act_spec, act_spec,
                  pl.BlockSpec((None, ts, d), lambda bi, si: (bi, si, 0)),
                  _const_spec((1, e)), _const_spec((e, e)), _const_spec((1, e)),
                  _const_spec((e, d))],
        out_specs=pl.BlockSpec((None, ts, d), lambda bi, si: (bi, si, 0)),
        compiler_params=_params("parallel", "parallel"),
        name="l1_glu_out",
    )(y, u, z, x1, d_skip.reshape(1, e), wg_bf, b_glu.reshape(1, e), wo_bf)


def _s5_tiles(lam_re, lam_im, log_dt, b_re, b_im, c_re, c_im):
    g, p = lam_re.shape
    ch = b_re.shape[2]
    nt = g // SSM_TILE_GROUPS
    dt = jnp.exp(log_dt.astype(F32))[:, None]
    lr, li = lam_re.astype(F32), lam_im.astype(F32)
    mag = jnp.exp(lr * dt)
    ab_re, ab_im = mag * jnp.cos(li * dt), mag * jnp.sin(li * dt)
    den = lr * lr + li * li
    nr, ni = ab_re - 1.0, ab_im
    g_re = (nr * lr + ni * li) / den
    g_im = (ni * lr - nr * li) / den
    br, bi = b_re.astype(F32), b_im.astype(F32)
    bb_re = g_re[..., None] * br - g_im[..., None] * bi
    bb_im = g_re[..., None] * bi + g_im[..., None] * br
    eye = jnp.eye(SSM_TILE_GROUPS, dtype=BF16)

    def b_tile(bb):
        t = bb.astype(BF16).reshape(nt, SSM_TILE_GROUPS, p, ch)
        t = jnp.einsum("ngpc,gh->ngchp", t, eye)
        return t.reshape(nt, SSM_TILE_GROUPS * ch, SSM_TILE_GROUPS * p)

    def c_tile(cc):
        t = cc.astype(BF16).reshape(nt, SSM_TILE_GROUPS, ch, p)
        t = jnp.einsum("ngcp,gh->ngphc", t, eye)
        return t.reshape(nt, SSM_TILE_GROUPS * p, SSM_TILE_GROUPS * ch)

    bt = jnp.concatenate([b_tile(bb_re), b_tile(bb_im)], axis=2)
    ct = jnp.concatenate([c_tile(c_re.astype(F32)), c_tile(-c_im.astype(F32))], axis=1)
    a_tiles = jnp.stack([ab_re.reshape(nt, SSM_TILE_STATE), ab_im.reshape(nt, SSM_TILE_STATE)], axis=1)
    return bt, ct, a_tiles


def kernel(x, l0_norm_g, l0_w_in, l0_q_norm_g, l0_k_norm_g, l0_lam_q1, l0_lam_k1, l0_lam_q2, l0_lam_k2, l0_head_norm_g, l0_w_out, l1_norm_g, l1_w_in, l1_lam_re, l1_lam_im, l1_log_dt, l1_b_re, l1_b_im, l1_c_re, l1_c_im, l1_d, l1_w_glu, l1_b_glu, l1_w_out):
    b, s, d = x.shape
    e = l0_w_out.shape[0]
    n_heads = e // HEAD_V
    assert s % LANES == 0 and e % MXU_DIM == 0 and b == SUBLANES

    gq = l0_q_norm_g.astype(F32) * (SUB_HEAD ** -0.5 * LOG2E)
    gk = l0_k_norm_g.astype(F32)
    reps = e // SUB_HEAD
    col_scale = jnp.concatenate([jnp.tile(gq, reps), jnp.tile(gk, reps)]).reshape(1, 2 * e)
    qk, vz = _l0_inproj(x.reshape(b * s, d), l0_norm_g.astype(F32), l0_w_in.astype(BF16), col_scale)

    score_bound = (BF16_ROUNDING_SLACK * SUB_HEAD * jnp.max(jnp.abs(gq)) * jnp.max(jnp.abs(gk))).reshape(1)
    lam_vecs = jnp.stack([l0_lam_q1, l0_lam_k1, l0_lam_q2, l0_lam_k2]).astype(F32)
    o = _l0_attn(qk.reshape(b, s, 2 * e), vz.reshape(b, s, 2 * e), lam_vecs,
                 l0_head_norm_g.astype(F32), score_bound, n_heads)

    x1, u, z = _l0_out_l1_in(o, x, l0_w_out.astype(BF16), l1_norm_g.astype(F32),
                             l1_w_in.astype(BF16))

    bt, ct, a_tiles = _s5_tiles(l1_lam_re, l1_lam_im, l1_log_dt, l1_b_re, l1_b_im, l1_c_re, l1_c_im)
    y = _l1_ssm(u, bt, ct, a_tiles)

    return _l1_glu_out(y, u, z, x1, l1_d.astype(F32), l1_w_glu.astype(BF16),
                       l1_b_glu.astype(F32), l1_w_out.astype(BF16))
```

```python
import functools
import math

import jax
import jax.numpy as jnp
import numpy as np
from jax import lax
from jax.experimental import pallas as pl
from jax.experimental.pallas import tpu as pltpu

F32 = jnp.float32
BF16 = jnp.bfloat16

EPS = 1e-6
SUB_HEAD = 64
HEAD_V = 2 * SUB_HEAD
GROUP_CH = 16
GROUP_STATE = 64
LAMBDA_INIT_L0 = 0.8 - 0.6 * math.exp(-0.3 * 0)
LOG2E = math.log2(math.e)

LANES = 128
SUBLANES = 8
MXU_DIM = 256
VMEM_LIMIT_BYTES = 56 * 1024 * 1024
NEG_BIG = -1e30
F32_MIN_EXP = 126
SUM_HEADROOM_BITS = 26

INPROJ_TM = 1024
INPROJ_TN = 1024
OUT_IN_ROWS = 512
SSM_CHUNK_STEPS = 128
ATTN_TQ = 512
ATTN_TQ_ONLINE = 256
FIXED_SHIFT_MAX_LOG2 = (F32_MIN_EXP - SUM_HEADROOM_BITS) / 2.0
BF16_ROUNDING_SLACK = 1.01

SSM_TILE_CH = MXU_DIM
SSM_TILE_GROUPS = SSM_TILE_CH // GROUP_CH
SSM_TILE_STATE = SSM_TILE_GROUPS * GROUP_STATE
SCAN_LANES = 256
GLU_ROWS = 512
GLU_ROW_SPLITS = 2
SSM_STAGES = 3
SSM_ROW_SPLITS = 2


def _params(*sem):
    return pltpu.CompilerParams(dimension_semantics=sem, vmem_limit_bytes=VMEM_LIMIT_BYTES)


def _const_spec(shape):
    nd = len(shape)
    return pl.BlockSpec(shape, lambda *_: (0,) * nd, pipeline_mode=pl.Buffered(1))


def _pick(n, pref):
    t = min(n, pref)
    while n % t:
        t //= 2
    return t


def _l0_inproj_kernel(x_ref, g_ref, wqk_ref, wvz_ref, cs_ref, e_ref, qk_ref, vz_ref):
    xf = x_ref[...]
    ms = jnp.mean(xf * xf, axis=-1, keepdims=True)
    hn = (xf * lax.rsqrt(ms + EPS) * g_ref[...]).astype(BF16)
    acc = jnp.dot(hn, wqk_ref[...], preferred_element_type=F32)
    sq = acc * acc
    hi = sq.astype(BF16)
    lo = (sq - hi.astype(F32)).astype(BF16)
    e = e_ref[...]
    for c in range(acc.shape[1] // MXU_DIM):
        sl = slice(c * MXU_DIM, (c + 1) * MXU_DIM)
        ss = (jnp.dot(hi[:, sl], e, preferred_element_type=F32)
              + jnp.dot(lo[:, sl], e, preferred_element_type=F32))
        inv = lax.rsqrt(ss * (1.0 / SUB_HEAD) + EPS)
        qk_ref[:, sl] = (acc[:, sl] * inv * cs_ref[:, sl]).astype(BF16)
    vz_ref[...] = jnp.dot(hn, wvz_ref[...], preferred_element_type=F32).astype(BF16)


def _l0_inproj(x2d, norm_g, w_bf, col_scale):
    m, d = x2d.shape
    n_half = w_bf.shape[1] // 2
    tm = _pick(m, INPROJ_TM)
    tn = _pick(n_half, INPROJ_TN)
    n_col_tiles = n_half // tn
    seg = np.arange(MXU_DIM) // SUB_HEAD
    e = jnp.asarray(seg[:, None] == seg[None, :], dtype=BF16)
    out = jax.ShapeDtypeStruct((m, n_half), BF16)
    return pl.pallas_call(
        _l0_inproj_kernel,
        out_shape=(out, out),
        grid=(m // tm, n_col_tiles),
        in_specs=[
            pl.BlockSpec((tm, d), lambda i, j: (i, 0)),
            _const_spec((1, d)),
            pl.BlockSpec((d, tn), lambda i, j: (0, j)),
            pl.BlockSpec((d, tn), lambda i, j: (0, j + n_col_tiles)),
            pl.BlockSpec((1, tn), lambda i, j: (0, j)),
            _const_spec((MXU_DIM, MXU_DIM)),
        ],
        out_specs=(pl.BlockSpec((tm, tn), lambda i, j: (i, j)),
                   pl.BlockSpec((tm, tn), lambda i, j: (i, j))),
        compiler_params=_params("parallel", "parallel"),
        name="l0_inproj",
    )(x2d, norm_g.reshape(1, d), w_bf, w_bf, col_scale, e)


def _split3(val):
    p1 = val.astype(BF16).astype(F32)
    r1 = val - p1
    p2 = r1.astype(BF16).astype(F32)
    p3 = (r1 - p2).astype(BF16).astype(F32)
    return p1, p2, p3


def _augment(parts, lane, off, ones_first):
    p_off = off + 3 if ones_first else off
    o_off = off if ones_first else off + 3
    a = jnp.where(lane == p_off, parts[0], 0.0)
    a = jnp.where(lane == p_off + 1, parts[1], a)
    a = jnp.where(lane == p_off + 2, parts[2], a)
    for t in range(3):
        a = jnp.where(lane == o_off + t, 1.0, a)
    return a


def _scores(qa, kk):
    return lax.dot_general(qa, kk, (((1,), (1,)), ((), ())), preferred_element_type=F32)


def _attn_kernel(shift_ref, q_ref, k_ref, v_ref, z_ref, lam_ref, hg_ref, slope_ref, o_ref,
                 aq_scr, ak_scr, qaug_scr, kaug_scr, vaug_scr, acc_scr, acc1_scr, m_scr, l_scr,
                 *, tq, tq1):
    s_len = q_ref.shape[0]
    n_blk = s_len // tq
    half = tq // 2
    h = pl.program_id(0)
    shift = shift_ref[0]
    slope = slope_ref[pl.ds(h, 1), :] * LOG2E
    lane = lax.broadcasted_iota(jnp.int32, (tq, LANES), 1)
    row = lax.broadcasted_iota(jnp.int32, (tq, LANES), 0)

    lam_v = lam_ref[...]
    lam = (jnp.exp(jnp.sum(lam_v[0:1] * lam_v[1:2], axis=1, keepdims=True))
           - jnp.exp(jnp.sum(lam_v[2:3] * lam_v[3:4], axis=1, keepdims=True)) + LAMBDA_INIT_L0)

    def blk(i, size):
        return pl.ds(pl.multiple_of(i * size, size), size)

    def finish(rows, o0, l0, o1, l1):
        o = o0 * (1.0 / l0) - lam * (o1 * (1.0 / l1))
        ms = jnp.mean(o * o, axis=-1, keepdims=True)
        on = o * lax.rsqrt(ms + EPS) * hg_ref[...] * (1.0 - LAMBDA_INIT_L0)
        z = z_ref[rows, :].astype(F32)
        o_ref[rows, :] = (on * (z / (1.0 + jnp.exp(-z)))).astype(BF16)

    @pl.when(pl.program_id(1) == 0)
    def _tables():
        ones_col = jnp.ones((tq, HEAD_V), BF16)

        def body(c, carry):
            rows = blk(c, tq)
            pos = slope * (c * tq + row).astype(F32)
            k_parts = _split3(pos)
            q_parts = _split3(-(pos + shift))
            ak_scr[0, rows, :] = _augment(k_parts, lane, SUB_HEAD, True).astype(BF16)
            ak_scr[1, rows, :] = _augment(k_parts, lane, 0, True).astype(BF16)
            aq_scr[0, rows, :] = _augment(q_parts, lane, SUB_HEAD, False).astype(BF16)
            aq_scr[1, rows, :] = _augment(q_parts, lane, 0, False).astype(BF16)
            vaug_scr[rows, HEAD_V:2 * HEAD_V] = ones_col
            return carry

        lax.fori_loop(0, n_blk, body, 0)

    lane_row = lax.broadcasted_iota(jnp.int32, (1, LANES), 1)
    keep = (jnp.where(lane_row < SUB_HEAD, 1.0, 0.0).astype(BF16),
            jnp.where(lane_row < SUB_HEAD, 0.0, 1.0).astype(BF16))

    def build(rows):
        kb = k_ref[rows, :]
        qb = q_ref[rows, :]
        for sub in range(2):
            kaug_scr[sub, rows, :] = kb * keep[sub] + ak_scr[sub, rows, :]
            qaug_scr[sub, rows, :] = qb * keep[sub] + aq_scr[sub, rows, :]
        vaug_scr[rows, 0:HEAD_V] = v_ref[rows, :]

    fixed_shift_ok = shift <= FIXED_SHIFT_MAX_LOG2

    @pl.when(fixed_shift_ok)
    def _fixed_shift():
        def tile(q_rows, k_rows, mask_off, init):
            n_q, n_k = q_rows.size, k_rows.size
            for sub in range(2):
                s = _scores(qaug_scr[sub, q_rows, :], kaug_scr[sub, k_rows, :])
                if mask_off is not None:
                    cols = lax.broadcasted_iota(jnp.int32, (n_q, n_k), 1)
                    rws = lax.broadcasted_iota(jnp.int32, (n_q, n_k), 0)
                    s = jnp.where(cols <= rws + mask_off, s, -jnp.inf)
                pv = jnp.dot(jnp.exp2(s).astype(BF16), vaug_scr[k_rows, :], preferred_element_type=F32)
                if init:
                    acc_scr[sub, q_rows, :] = pv
                else:
                    acc_scr[sub, q_rows, :] += pv

        for i in range(n_blk):
            r0 = i * tq
            build(pl.ds(r0, tq))
            tile(pl.ds(r0, half), pl.ds(r0, half), 0, True)
            tile(pl.ds(r0 + half, half), pl.ds(r0, tq), half, True)
            for j in range(i):
                tile(pl.ds(r0, tq), pl.ds(j * tq, tq), None, False)
            rows = pl.ds(r0, tq)
            a0 = acc_scr[0, rows, :]
            a1 = acc_scr[1, rows, :]
            finish(rows, a0[:, :HEAD_V], a0[:, HEAD_V:], a1[:, :HEAD_V], a1[:, HEAD_V:])

    @pl.when(jnp.logical_not(fixed_shift_ok))
    def _online():
        col_s = lax.broadcasted_iota(jnp.int32, (tq1, tq1), 1)
        row_s = lax.broadcasted_iota(jnp.int32, (tq1, tq1), 0)
        causal = col_s <= row_s

        def build_block(c, carry):
            build(blk(c, tq))
            return carry

        lax.fori_loop(0, n_blk, build_block, 0)

        def q_block(i, carry):
            rows = blk(i, tq1)
            m_scr[...] = jnp.full(m_scr.shape, NEG_BIG, F32)
            l_scr[...] = jnp.zeros(l_scr.shape, F32)
            acc1_scr[...] = jnp.zeros(acc1_scr.shape, F32)

            def kv_step(j, masked):
                k_rows = blk(j, tq1)
                v = v_ref[k_rows, :]
                for sub in range(2):
                    s = _scores(qaug_scr[sub, rows, :], kaug_scr[sub, k_rows, :])
                    if masked:
                        s = jnp.where(causal, s, -jnp.inf)
                    m_prev = m_scr[sub]
                    m_new = jnp.maximum(m_prev, jnp.max(s, axis=1, keepdims=True))
                    alpha = jnp.exp2(m_prev - m_new)
                    p = jnp.exp2(s - m_new)
                    l_scr[sub] = alpha * l_scr[sub] + jnp.sum(p, axis=1, keepdims=True)
                    acc1_scr[sub] = alpha * acc1_scr[sub] + jnp.dot(
                        p.astype(BF16), v, preferred_element_type=F32)
                    m_scr[sub] = m_new

            def off_diag(j, c):
                kv_step(j, False)
                return c

            lax.fori_loop(0, i, off_diag, 0)
            kv_step(i, True)
            finish(rows, acc1_scr[0], l_scr[0], acc1_scr[1], l_scr[1])
            return carry

        lax.fori_loop(0, s_len // tq1, q_block, 0)


def _l0_attn(qk, vz, lam_vecs, head_g, score_bound, n_heads):
    b, s, _ = qk.shape
    tq = _pick(s, ATTN_TQ)
    tq1 = _pick(s, ATTN_TQ_ONLINE)
    slopes = 2.0 ** (-8.0 * np.arange(1, n_heads + 1) / n_heads)
    slope_tab = jnp.asarray(np.broadcast_to(slopes[:, None], (n_heads, LANES)), dtype=F32)

    def head_block(which):
        return pl.BlockSpec((None, s, HEAD_V), lambda hi, bi: (bi, 0, which * n_heads + hi))

    smem = pl.BlockSpec(memory_space=pltpu.SMEM)
    return pl.pallas_call(
        functools.partial(_attn_kernel, tq=tq, tq1=tq1),
        out_shape=jax.ShapeDtypeStruct((b, s, n_heads * HEAD_V), BF16),
        grid=(n_heads, b),
        in_specs=[smem, head_block(0), head_block(1), head_block(0), head_block(1),
                  _const_spec((4, SUB_HEAD)), _const_spec((1, HEAD_V)),
                  _const_spec((n_heads, LANES))],
        out_specs=pl.BlockSpec((None, s, HEAD_V), lambda hi, bi: (bi, 0, hi)),
        scratch_shapes=[pltpu.VMEM((2, s, HEAD_V), BF16),
                        pltpu.VMEM((2, s, HEAD_V), BF16),
                        pltpu.VMEM((2, s, HEAD_V), BF16),
                        pltpu.VMEM((2, s, HEAD_V), BF16),
                        pltpu.VMEM((s, 2 * HEAD_V), BF16),
                        pltpu.VMEM((2, s, 2 * HEAD_V), F32),
                        pltpu.VMEM((2, tq1, HEAD_V), F32),
                        pltpu.VMEM((2, tq1, 1), F32),
                        pltpu.VMEM((2, tq1, 1), F32)],
        compiler_params=_params("parallel", "arbitrary"),
        name="l0_attn",
    )(score_bound, qk, qk, vz, vz, lam_vecs, head_g.reshape(1, HEAD_V), slope_tab)


def _l0_out_l1_in_kernel(o_ref, x_ref, wo_ref, g_ref, wi_ref, x1_ref, u_ref, z_ref):
    e = u_ref.shape[1]
    x1 = x_ref[...] + jnp.dot(o_ref[...], wo_ref[...], preferred_element_type=F32)
    x1_ref[...] = x1
    ms = jnp.mean(x1 * x1, axis=-1, keepdims=True)
    hn = (x1 * lax.rsqrt(ms + EPS) * g_ref[...]).astype(BF16)
    u_ref[...] = jnp.dot(hn, wi_ref[:, :e], preferred_element_type=F32).astype(BF16)
    z_ref[...] = jnp.dot(hn, wi_ref[:, e:], preferred_element_type=F32).astype(BF16)


def _l0_out_l1_in(o, x, wo_bf, norm_g, wi_bf):
    b, s, d = x.shape
    e = o.shape[2]
    tm = _pick(s, OUT_IN_ROWS)
    act_shape = jax.ShapeDtypeStruct((b, s, e), BF16)
    return pl.pallas_call(
        _l0_out_l1_in_kernel,
        out_shape=(jax.ShapeDtypeStruct((b, s, d), F32), act_shape, act_shape),
        grid=(b, s // tm),
        in_specs=[
            pl.BlockSpec((None, tm, e), lambda bi, si: (bi, si, 0)),
            pl.BlockSpec((None, tm, d), lambda bi, si: (bi, si, 0)),
            _const_spec((e, d)),
            _const_spec((1, d)),
            _const_spec((d, 2 * e)),
        ],
        out_specs=(
            pl.BlockSpec((None, tm, d), lambda bi, si: (bi, si, 0)),
            pl.BlockSpec((None, tm, e), lambda bi, si: (bi, si, 0)),
            pl.BlockSpec((None, tm, e), lambda bi, si: (bi, si, 0)),
        ),
        compiler_params=_params("parallel", "parallel"),
        name="l0_out_l1_in",
    )(o, x, wo_bf, norm_g.reshape(1, d), wi_bf)


def _ssm_kernel(u_ref, bt_ref, a_ref, ct_ref, y_ref, il_in_scr, il_out_scr, st0_scr, st1_scr, st2_scr, h_scr,
                *, n_chunks):
    batch, n_steps, _ = u_ref.shape
    n_slabs = il_in_scr.shape[0]
    half = batch * n_steps // SSM_ROW_SPLITS
    f = pl.program_id(0)
    st_scrs = (st0_scr, st1_scr, st2_scr)

    @pl.when(f == 0)
    def _():
        for st in st_scrs:
            st[...] = jnp.zeros(st.shape, F32)
        h_scr[...] = jnp.zeros(h_scr.shape, F32)

    keep = jnp.where((f - 1) % n_chunks == 0, 0.0, 1.0).astype(F32)

    def stages(st_b, st_s, st_c):
        for b in range(batch):
            ub = u_ref[b].astype(F32)
            for k in range(n_slabs):
                il_in_scr[k, pl.ds(b, n_steps, stride=batch), :] = ub[:, k * LANES:(k + 1) * LANES]
        for r in range(SSM_ROW_SPLITS):
            rs = slice(r * half, (r + 1) * half)
            u_il = jnp.concatenate([il_in_scr[k, rs, :] for k in range(n_slabs)], axis=1).astype(BF16)
            st_b[rs, :] = jnp.dot(u_il, bt_ref[...], preferred_element_type=F32)

        for w in range(SSM_TILE_STATE // SCAN_LANES):
            re = pl.ds(w * SCAN_LANES, SCAN_LANES)
            im = pl.ds(SSM_TILE_STATE + w * SCAN_LANES, SCAN_LANES)
            a_re = jnp.broadcast_to(a_ref[0:1, re], (batch, SCAN_LANES))
            a_im = jnp.broadcast_to(a_ref[1:2, re], (batch, SCAN_LANES))
            h_re, h_im = h_scr[:, re] * keep, h_scr[:, im] * keep
            for t in range(n_steps):
                rows = pl.ds(t * batch, batch)
                h_re, h_im = (a_re * h_re - a_im * h_im + st_s[rows, re],
                              a_re * h_im + a_im * h_re + st_s[rows, im])
                st_s[rows, re] = h_re
                st_s[rows, im] = h_im
            h_scr[:, re] = h_re
            h_scr[:, im] = h_im

        for r in range(SSM_ROW_SPLITS):
            rs = slice(r * half, (r + 1) * half)
            y = jnp.dot(st_c[rs, :].astype(BF16), ct_ref[...], preferred_element_type=F32)
            for k in range(n_slabs):
                il_out_scr[k, rs, :] = y[:, k * LANES:(k + 1) * LANES]
        for b in range(batch):
            for k in range(n_slabs):
                y_ref[b, :, k * LANES:(k + 1) * LANES] = il_out_scr[
                    k, pl.ds(b, n_steps, stride=batch), :].astype(BF16)

    for res in range(SSM_STAGES):
        pl.when(f % SSM_STAGES == res)(functools.partial(
            stages, st_scrs[res], st_scrs[(res + 2) % SSM_STAGES], st_scrs[(res + 1) % SSM_STAGES]))


def _l1_ssm(u, bt, ct, a_tiles):
    batch, s, e = u.shape
    n_tiles = e // SSM_TILE_CH
    steps = _pick(s, SSM_CHUNK_STEPS)
    rows = steps * batch
    n_chunks = s // steps
    n_items = n_tiles * n_chunks
    blk_shape = (batch, steps, SSM_TILE_CH)
    il_scratch = pltpu.VMEM((SSM_TILE_CH // LANES, rows, LANES), F32)
    st_scratch = pltpu.VMEM((rows, 2 * SSM_TILE_STATE), F32)

    def item(f, lag):
        return jnp.clip(f - lag, 0, n_items - 1)

    def act_map(lag):
        return lambda f: (0, item(f, lag) % n_chunks, item(f, lag) // n_chunks)

    def tile_map(lag):
        return lambda f: (item(f, lag) // n_chunks, 0, 0)

    return pl.pallas_call(
        functools.partial(_ssm_kernel, n_chunks=n_chunks),
        out_shape=jax.ShapeDtypeStruct((batch, s, e), BF16),
        grid=(n_items + SSM_STAGES - 1,),
        in_specs=[
            pl.BlockSpec(blk_shape, act_map(0)),
            pl.BlockSpec((None, SSM_TILE_CH, 2 * SSM_TILE_STATE), tile_map(0)),
            pl.BlockSpec((None, 2, SSM_TILE_STATE), tile_map(1)),
            pl.BlockSpec((None, 2 * SSM_TILE_STATE, SSM_TILE_CH), tile_map(2)),
        ],
        out_specs=pl.BlockSpec(blk_shape, act_map(SSM_STAGES - 1)),
        scratch_shapes=[il_scratch, il_scratch, st_scratch, st_scratch, st_scratch,
                        pltpu.VMEM((batch, 2 * SSM_TILE_STATE), F32)],
        compiler_params=_params("arbitrary"),
        name="l1_ssm",
    )(u, bt, a_tiles, ct)


def _glu_out_kernel(y_ref, u_ref, z_ref, x1_ref, d_ref, wg_ref, bg_ref, wo_ref, o_ref):
    rows = y_ref.shape[0] // GLU_ROW_SPLITS
    for r in range(GLU_ROW_SPLITS):
        rs = slice(r * rows, (r + 1) * rows)
        y = y_ref[rs, :].astype(F32) + d_ref[...] * u_ref[rs, :].astype(F32)
        cdf = 0.5 * (1.0 + jnp.tanh(math.sqrt(2.0 / math.pi) * (y + 0.044715 * (y * y * y))))
        y = y * cdf
        gate = jnp.dot(y.astype(BF16), wg_ref[...], preferred_element_type=F32) + bg_ref[...]
        y = y * (1.0 / (1.0 + jnp.exp(-gate)))
        z = z_ref[rs, :].astype(F32)
        y = y * (z / (1.0 + jnp.exp(-z)))
        o_ref[rs, :] = x1_ref[rs, :] + jnp.dot(y.astype(BF16), wo_ref[...], preferred_element_type=F32)


def _l1_glu_out(y, u, z, x1, d_skip, wg_bf, b_glu, wo_bf):
    b, s, d = x1.shape
    e = wg_bf.shape[0]
    ts = _pick(s, GLU_ROWS)
    act_spec = pl.BlockSpec((None, ts, e), lambda bi, si: (bi, si, 0))
    return pl.pallas_call(
        _glu_out_kernel,
        out_shape=jax.ShapeDtypeStruct((b, s, d), F32),
        grid=(b, s // ts),
        in_specs=[act_spec, act_spec, act_spec,
                  pl.BlockSpec((None, ts, d), lambda bi, si: (bi, si, 0)),
                  _const_spec((1, e)), _const_spec((e, e)), _const_spec((1, e)),
                  _const_spec((e, d))],
        out_specs=pl.BlockSpec((None, ts, d), lambda bi, si: (bi, si, 0)),
        compiler_params=_params("parallel", "parallel"),
        name="l1_glu_out",
    )(y, u, z, x1, d_skip.reshape(1, e), wg_bf, b_glu.reshape(1, e), wo_bf)


def _s5_tiles(lam_re, lam_im, log_dt, b_re, b_im, c_re, c_im):
    g, p = lam_re.shape
    ch = b_re.shape[2]
    nt = g // SSM_TILE_GROUPS
    dt = jnp.exp(log_dt.astype(F32))[:, None]
    lr, li = lam_re.astype(F32), lam_im.astype(F32)
    mag = jnp.exp(lr * dt)
    ab_re, ab_im = mag * jnp.cos(li * dt), mag * jnp.sin(li * dt)
    den = lr * lr + li * li
    nr, ni = ab_re - 1.0, ab_im
    g_re = (nr * lr + ni * li) / den
    g_im = (ni * lr - nr * li) / den
    br, bi = b_re.astype(F32), b_im.astype(F32)
    bb_re = g_re[..., None] * br - g_im[..., None] * bi
    bb_im = g_re[..., None] * bi + g_im[..., None] * br
    eye = jnp.eye(SSM_TILE_GROUPS, dtype=BF16)

    def b_tile(bb):
        t = bb.astype(BF16).reshape(nt, SSM_TILE_GROUPS, p, ch)
        t = jnp.einsum("ngpc,gh->ngchp", t, eye)
        return t.reshape(nt, SSM_TILE_GROUPS * ch, SSM_TILE_GROUPS * p)

    def c_tile(cc):
        t = cc.astype(BF16).reshape(nt, SSM_TILE_GROUPS, ch, p)
        t = jnp.einsum("ngcp,gh->ngphc", t, eye)
        return t.reshape(nt, SSM_TILE_GROUPS * p, SSM_TILE_GROUPS * ch)

    bt = jnp.concatenate([b_tile(bb_re), b_tile(bb_im)], axis=2)
    ct = jnp.concatenate([c_tile(c_re.astype(F32)), c_tile(-c_im.astype(F32))], axis=1)
    a_tiles = jnp.stack([ab_re.reshape(nt, SSM_TILE_STATE), ab_im.reshape(nt, SSM_TILE_STATE)], axis=1)
    return bt, ct, a_tiles


def kernel(x, l0_norm_g, l0_w_in, l0_q_norm_g, l0_k_norm_g, l0_lam_q1, l0_lam_k1, l0_lam_q2, l0_lam_k2, l0_head_norm_g, l0_w_out, l1_norm_g, l1_w_in, l1_lam_re, l1_lam_im, l1_log_dt, l1_b_re, l1_b_im, l1_c_re, l1_c_im, l1_d, l1_w_glu, l1_b_glu, l1_w_out):
    b, s, d = x.shape
    e = l0_w_out.shape[0]
    n_heads = e // HEAD_V
    assert s % LANES == 0 and e % MXU_DIM == 0 and b == SUBLANES

    gq = l0_q_norm_g.astype(F32) * (SUB_HEAD ** -0.5 * LOG2E)
    gk = l0_k_norm_g.astype(F32)
    reps = e // SUB_HEAD
    col_scale = jnp.concatenate([jnp.tile(gq, reps), jnp.tile(gk, reps)]).reshape(1, 2 * e)
    qk, vz = _l0_inproj(x.reshape(b * s, d), l0_norm_g.astype(F32), l0_w_in.astype(BF16), col_scale)

    score_bound = (BF16_ROUNDING_SLACK * SUB_HEAD * jnp.max(jnp.abs(gq)) * jnp.max(jnp.abs(gk))).reshape(1)
    lam_vecs = jnp.stack([l0_lam_q1, l0_lam_k1, l0_lam_q2, l0_lam_k2]).astype(F32)
    o = _l0_attn(qk.reshape(b, s, 2 * e), vz.reshape(b, s, 2 * e), lam_vecs,
                 l0_head_norm_g.astype(F32), score_bound, n_heads)

    x1, u, z = _l0_out_l1_in(o, x, l0_w_out.astype(BF16), l1_norm_g.astype(F32),
                             l1_w_in.astype(BF16))

    bt, ct, a_tiles = _s5_tiles(l1_lam_re, l1_lam_im, l1_log_dt, l1_b_re, l1_b_im, l1_c_re, l1_c_im)
    y = _l1_ssm(u, bt, ct, a_tiles)

    return _l1_glu_out(y, u, z, x1, l1_d.astype(F32), l1_w_glu.astype(BF16),
                       l1_b_glu.astype(F32), l1_w_out.astype(BF16))
```

```python
import functools
import math

import jax
import jax.numpy as jnp
import numpy as np
from jax import lax
from jax.experimental import pallas as pl
from jax.experimental.pallas import tpu as pltpu

F32 = jnp.float32
BF16 = jnp.bfloat16

EPS = 1e-6
SUB_HEAD = 64
HEAD_V = 2 * SUB_HEAD
GROUP_CH = 16
GROUP_STATE = 64
LAMBDA_INIT_L0 = 0.8 - 0.6 * math.exp(-0.3 * 0)
LOG2E = math.log2(math.e)

LANES = 128
SUBLANES = 8
MXU_DIM = 256
VMEM_LIMIT_BYTES = 56 * 1024 * 1024
NEG_BIG = -1e30
F32_MIN_EXP = 126
SUM_HEADROOM_BITS = 26

INPROJ_TM = 1024
INPROJ_TN = 1024
OUT_IN_ROWS = 512
SSM_CHUNK_STEPS = 128
ATTN_TQ = 512
ATTN_TQ_ONLINE = 256
FIXED_SHIFT_MAX_LOG2 = (F32_MIN_EXP - SUM_HEADROOM_BITS) / 2.0
BF16_ROUNDING_SLACK = 1.01

SSM_TILE_CH = MXU_DIM
SSM_TILE_GROUPS = SSM_TILE_CH // GROUP_CH
SSM_TILE_STATE = SSM_TILE_GROUPS * GROUP_STATE
SCAN_LANES = 256
GLU_ROWS = 512
GLU_ROW_SPLITS = 2
SSM_STAGES = 3
SSM_ROW_SPLITS = 2


def _params(*sem):
    return pltpu.CompilerParams(dimension_semantics=sem, vmem_limit_bytes=VMEM_LIMIT_BYTES)


def _const_spec(shape):
    nd = len(shape)
    return pl.BlockSpec(shape, lambda *_: (0,) * nd, pipeline_mode=pl.Buffered(1))


def _pick(n, pref):
    t = min(n, pref)
    while n % t:
        t //= 2
    return t


def _l0_inproj_kernel(x_ref, g_ref, wqk_ref, wvz_ref, cs_ref, e_ref, qk_ref, vz_ref):
    xf = x_ref[...]
    ms = jnp.mean(xf * xf, axis=-1, keepdims=True)
    hn = (xf * lax.rsqrt(ms + EPS) * g_ref[...]).astype(BF16)
    acc = jnp.dot(hn, wqk_ref[...], preferred_element_type=F32)
    sq = acc * acc
    hi = sq.astype(BF16)
    lo = (sq - hi.astype(F32)).astype(BF16)
    e = e_ref[...]
    for c in range(acc.shape[1] // MXU_DIM):
        sl = slice(c * MXU_DIM, (c + 1) * MXU_DIM)
        ss = (jnp.dot(hi[:, sl], e, preferred_element_type=F32)
              + jnp.dot(lo[:, sl], e, preferred_element_type=F32))
        inv = lax.rsqrt(ss * (1.0 / SUB_HEAD) + EPS)
        _store_heads(qk_ref, c * (MXU_DIM // HEAD_V), (acc[:, sl] * inv * cs_ref[:, sl]).astype(BF16))
    _store_heads(vz_ref, 0, jnp.dot(hn, wvz_ref[...], preferred_element_type=F32).astype(BF16))


def _store_heads(ref, first, val):
    for h in range(val.shape[1] // HEAD_V):
        ref[first + h] = val[:, h * HEAD_V:(h + 1) * HEAD_V]


def _l0_inproj(x2d, seq, norm_g, w_bf, col_scale):
    m, d = x2d.shape
    n_half = w_bf.shape[1] // 2
    tm = _pick(seq, INPROJ_TM)
    tn = _pick(n_half, INPROJ_TN)
    n_col_tiles = n_half // tn
    seg = np.arange(MXU_DIM) // SUB_HEAD
    e = jnp.asarray(seg[:, None] == seg[None, :], dtype=BF16)
    assert seq % tm == 0 and tn % HEAD_V == 0
    n_si = seq // tm
    out = jax.ShapeDtypeStruct((m // seq, n_half // HEAD_V, seq, HEAD_V), BF16)
    out_spec = pl.BlockSpec((None, tn // HEAD_V, tm, HEAD_V), lambda i, j: (i // n_si, j, i % n_si, 0))
    return pl.pallas_call(
        _l0_inproj_kernel,
        out_shape=(out, out),
        grid=(m // tm, n_col_tiles),
        in_specs=[
            pl.BlockSpec((tm, d), lambda i, j: (i, 0)),
            _const_spec((1, d)),
            pl.BlockSpec((d, tn), lambda i, j: (0, j)),
            pl.BlockSpec((d, tn), lambda i, j: (0, j + n_col_tiles)),
            pl.BlockSpec((1, tn), lambda i, j: (0, j)),
            _const_spec((MXU_DIM, MXU_DIM)),
        ],
        out_specs=(out_spec, out_spec),
        compiler_params=_params("parallel", "parallel"),
        name="l0_inproj",
    )(x2d, norm_g.reshape(1, d), w_bf, w_bf, col_scale, e)


def _split3(val):
    p1 = val.astype(BF16).astype(F32)
    r1 = val - p1
    p2 = r1.astype(BF16).astype(F32)
    p3 = (r1 - p2).astype(BF16).astype(F32)
    return p1, p2, p3


def _augment(parts, lane, off, ones_first):
    p_off = off + 3 if ones_first else off
    o_off = off if ones_first else off + 3
    a = jnp.where(lane == p_off, parts[0], 0.0)
    a = jnp.where(lane == p_off + 1, parts[1], a)
    a = jnp.where(lane == p_off + 2, parts[2], a)
    for t in range(3):
        a = jnp.where(lane == o_off + t, 1.0, a)
    return a


def _scores(qa, kk):
    return lax.dot_general(qa, kk, (((1,), (1,)), ((), ())), preferred_element_type=F32)


def _attn_kernel(shift_ref, q_ref, k_ref, v_ref, z_ref, lam_ref, hg_ref, slope_ref, o_ref,
                 aq_scr, ak_scr, qaug_scr, kaug_scr, vaug_scr, acc_scr, acc1_scr, m_scr, l_scr,
                 *, tq, tq1):
    s_len = q_ref.shape[0]
    n_blk = s_len // tq
    half = tq // 2
    h = pl.program_id(0)
    shift = shift_ref[0]
    slope = slope_ref[pl.ds(h, 1), :] * LOG2E
    lane = lax.broadcasted_iota(jnp.int32, (tq, LANES), 1)
    row = lax.broadcasted_iota(jnp.int32, (tq, LANES), 0)

    lam_v = lam_ref[...]
    lam = (jnp.exp(jnp.sum(lam_v[0:1] * lam_v[1:2], axis=1, keepdims=True))
           - jnp.exp(jnp.sum(lam_v[2:3] * lam_v[3:4], axis=1, keepdims=True)) + LAMBDA_INIT_L0)

    def blk(i, size):
        return pl.ds(pl.multiple_of(i * size, size), size)

    def finish(rows, o0, l0, o1, l1):
        o = o0 * (1.0 / l0) - lam * (o1 * (1.0 / l1))
        ms = jnp.mean(o * o, axis=-1, keepdims=True)
        on = o * lax.rsqrt(ms + EPS) * hg_ref[...] * (1.0 - LAMBDA_INIT_L0)
        z = z_ref[rows, :].astype(F32)
        o_ref[rows, :] = (on * (z / (1.0 + jnp.exp(-z)))).astype(BF16)

    @pl.when(pl.program_id(1) == 0)
    def _tables():
        ones_col = jnp.ones((tq, HEAD_V), BF16)

        def body(c, carry):
            rows = blk(c, tq)
            pos = slope * (c * tq + row).astype(F32)
            k_parts = _split3(pos)
            q_parts = _split3(-(pos + shift))
            ak_scr[0, rows, :] = _augment(k_parts, lane, SUB_HEAD, True).astype(BF16)
            ak_scr[1, rows, :] = _augment(k_parts, lane, 0, True).astype(BF16)
            aq_scr[0, rows, :] = _augment(q_parts, lane, SUB_HEAD, False).astype(BF16)
            aq_scr[1, rows, :] = _augment(q_parts, lane, 0, False).astype(BF16)
            vaug_scr[rows, HEAD_V:2 * HEAD_V] = ones_col
            return carry

        lax.fori_loop(0, n_blk, body, 0)

    lane_row = lax.broadcasted_iota(jnp.int32, (1, LANES), 1)
    keep = (jnp.where(lane_row < SUB_HEAD, 1.0, 0.0).astype(BF16),
            jnp.where(lane_row < SUB_HEAD, 0.0, 1.0).astype(BF16))

    def build(rows):
        kb = k_ref[rows, :]
        qb = q_ref[rows, :]
        for sub in range(2):
            kaug_scr[sub, rows, :] = kb * keep[sub] + ak_scr[sub, rows, :]
            qaug_scr[sub, rows, :] = qb * keep[sub] + aq_scr[sub, rows, :]
        vaug_scr[rows, 0:HEAD_V] = v_ref[rows, :]

    fixed_shift_ok = shift <= FIXED_SHIFT_MAX_LOG2

    @pl.when(fixed_shift_ok)
    def _fixed_shift():
        def tile(q_rows, k_rows, mask_off, init):
            n_q, n_k = q_rows.size, k_rows.size
            for sub in range(2):
                s = _scores(qaug_scr[sub, q_rows, :], kaug_scr[sub, k_rows, :])
                if mask_off is not None:
                    cols = lax.broadcasted_iota(jnp.int32, (n_q, n_k), 1)
                    rws = lax.broadcasted_iota(jnp.int32, (n_q, n_k), 0)
                    s = jnp.where(cols <= rws + mask_off, s, -jnp.inf)
                pv = jnp.dot(jnp.exp2(s).astype(BF16), vaug_scr[k_rows, :], preferred_element_type=F32)
                if init:
                    acc_scr[sub, q_rows, :] = pv
                else:
                    acc_scr[sub, q_rows, :] += pv

        for i in range(n_blk):
            r0 = i * tq
            build(pl.ds(r0, tq))
            tile(pl.ds(r0, half), pl.ds(r0, half), 0, True)
            tile(pl.ds(r0 + half, half), pl.ds(r0, tq), half, True)
            for j in range(i):
                tile(pl.ds(r0, tq), pl.ds(j * tq, tq), None, False)
            rows = pl.ds(r0, tq)
            a0 = acc_scr[0, rows, :]
            a1 = acc_scr[1, rows, :]
            finish(rows, a0[:, :HEAD_V], a0[:, HEAD_V:], a1[:, :HEAD_V], a1[:, HEAD_V:])

    @pl.when(jnp.logical_not(fixed_shift_ok))
    def _online():
        col_s = lax.broadcasted_iota(jnp.int32, (tq1, tq1), 1)
        row_s = lax.broadcasted_iota(jnp.int32, (tq1, tq1), 0)
        causal = col_s <= row_s

        def build_block(c, carry):
            build(blk(c, tq))
            return carry

        lax.fori_loop(0, n_blk, build_block, 0)

        def q_block(i, carry):
            rows = blk(i, tq1)
            m_scr[...] = jnp.full(m_scr.shape, NEG_BIG, F32)
            l_scr[...] = jnp.zeros(l_scr.shape, F32)
            acc1_scr[...] = jnp.zeros(acc1_scr.shape, F32)

            def kv_step(j, masked):
                k_rows = blk(j, tq1)
                v = v_ref[k_rows, :]
                for sub in range(2):
                    s = _scores(qaug_scr[sub, rows, :], kaug_scr[sub, k_rows, :])
                    if masked:
                        s = jnp.where(causal, s, -jnp.inf)
                    m_prev = m_scr[sub]
                    m_new = jnp.maximum(m_prev, jnp.max(s, axis=1, keepdims=True))
                    alpha = jnp.exp2(m_prev - m_new)
                    p = jnp.exp2(s - m_new)
                    l_scr[sub] = alpha * l_scr[sub] + jnp.sum(p, axis=1, keepdims=True)
                    acc1_scr[sub] = alpha * acc1_scr[sub] + jnp.dot(
                        p.astype(BF16), v, preferred_element_type=F32)
                    m_scr[sub] = m_new

            def off_diag(j, c):
                kv_step(j, False)
                return c

            lax.fori_loop(0, i, off_diag, 0)
            kv_step(i, True)
            finish(rows, acc1_scr[0], l_scr[0], acc1_scr[1], l_scr[1])
            return carry

        lax.fori_loop(0, s_len // tq1, q_block, 0)


def _l0_attn(qk, vz, lam_vecs, head_g, score_bound, n_heads):
    b, _, s, _ = qk.shape
    tq = _pick(s, ATTN_TQ)
    tq1 = _pick(s, ATTN_TQ_ONLINE)
    slopes = 2.0 ** (-8.0 * np.arange(1, n_heads + 1) / n_heads)
    slope_tab = jnp.asarray(np.broadcast_to(slopes[:, None], (n_heads, LANES)), dtype=F32)

    def head_block(which):
        return pl.BlockSpec((None, None, s, HEAD_V), lambda hi, bi: (bi, which * n_heads + hi, 0, 0))

    smem = pl.BlockSpec(memory_space=pltpu.SMEM)
    return pl.pallas_call(
        functools.partial(_attn_kernel, tq=tq, tq1=tq1),
        out_shape=jax.ShapeDtypeStruct((b, n_heads, s, HEAD_V), BF16),
        grid=(n_heads, b),
        in_specs=[smem, head_block(0), head_block(1), head_block(0), head_block(1),
                  _const_spec((4, SUB_HEAD)), _const_spec((1, HEAD_V)),
                  _const_spec((n_heads, LANES))],
        out_specs=pl.BlockSpec((None, None, s, HEAD_V), lambda hi, bi: (bi, hi, 0, 0)),
        scratch_shapes=[pltpu.VMEM((2, s, HEAD_V), BF16),
                        pltpu.VMEM((2, s, HEAD_V), BF16),
                        pltpu.VMEM((2, s, HEAD_V), BF16),
                        pltpu.VMEM((2, s, HEAD_V), BF16),
                        pltpu.VMEM((s, 2 * HEAD_V), BF16),
                        pltpu.VMEM((2, s, 2 * HEAD_V), F32),
                        pltpu.VMEM((2, tq1, HEAD_V), F32),
                        pltpu.VMEM((2, tq1, 1), F32),
                        pltpu.VMEM((2, tq1, 1), F32)],
        compiler_params=_params("parallel", "arbitrary"),
        name="l0_attn",
    )(score_bound, qk, qk, vz, vz, lam_vecs, head_g.reshape(1, HEAD_V), slope_tab)


def _l0_out_l1_in_kernel(o_ref, x_ref, wo_ref, g_ref, wi_ref, x1_ref, u_ref, z_ref):
    e = u_ref.shape[1]
    o = jnp.concatenate([o_ref[h] for h in range(o_ref.shape[0])], axis=1)
    x1 = x_ref[...] + jnp.dot(o, wo_ref[...], preferred_element_type=F32)
    x1_ref[...] = x1
    ms = jnp.mean(x1 * x1, axis=-1, keepdims=True)
    hn = (x1 * lax.rsqrt(ms + EPS) * g_ref[...]).astype(BF16)
    u_ref[...] = jnp.dot(hn, wi_ref[:, :e], preferred_element_type=F32).astype(BF16)
    z_ref[...] = jnp.dot(hn, wi_ref[:, e:], preferred_element_type=F32).astype(BF16)


def _l0_out_l1_in(o, x, wo_bf, norm_g, wi_bf):
    b, s, d = x.shape
    n_heads = o.shape[1]
    e = n_heads * HEAD_V
    tm = _pick(s, OUT_IN_ROWS)
    act_shape = jax.ShapeDtypeStruct((b, s, e), BF16)
    return pl.pallas_call(
        _l0_out_l1_in_kernel,
        out_shape=(jax.ShapeDtypeStruct((b, s, d), F32), act_shape, act_shape),
        grid=(b, s // tm),
        in_specs=[
            pl.BlockSpec((None, n_heads, tm, HEAD_V), lambda bi, si: (bi, 0, si, 0)),
            pl.BlockSpec((None, tm, d), lambda bi, si: (bi, si, 0)),
            _const_spec((e, d)),
            _const_spec((1, d)),
            _const_spec((d, 2 * e)),
        ],
        out_specs=(
            pl.BlockSpec((None, tm, d), lambda bi, si: (bi, si, 0)),
            pl.BlockSpec((None, tm, e), lambda bi, si: (bi, si, 0)),
            pl.BlockSpec((None, tm, e), lambda bi, si: (bi, si, 0)),
        ),
        compiler_params=_params("parallel", "parallel"),
        name="l0_out_l1_in",
    )(o, x, wo_bf, norm_g.reshape(1, d), wi_bf)


def _ssm_kernel(u_ref, bt_ref, a_ref, ct_ref, y_ref, il_in_scr, il_out_scr, st0_scr, st1_scr, st2_scr, h_scr,
                *, n_chunks):
    batch, n_steps, _ = u_ref.shape
    n_slabs = il_in_scr.shape[0]
    half = batch * n_steps // SSM_ROW_SPLITS
    f = pl.program_id(0)
    st_scrs = (st0_scr, st1_scr, st2_scr)

    @pl.when(f == 0)
    def _():
        for st in st_scrs:
            st[...] = jnp.zeros(st.shape, F32)
        h_scr[...] = jnp.zeros(h_scr.shape, F32)

    keep = jnp.where((f - 1) % n_chunks == 0, 0.0, 1.0).astype(F32)

    def stages(st_b, st_s, st_c):
        for b in range(batch):
            ub = u_ref[b].astype(F32)
            for k in range(n_slabs):
                il_in_scr[k, pl.ds(b, n_steps, stride=batch), :] = ub[:, k * LANES:(k + 1) * LANES]
        for r in range(SSM_ROW_SPLITS):
            rs = slice(r * half, (r + 1) * half)
            u_il = jnp.concatenate([il_in_scr[k, rs, :] for k in range(n_slabs)], axis=1).astype(BF16)
            st_b[rs, :] = jnp.dot(u_il, bt_ref[...], preferred_element_type=F32)

        for w in range(SSM_TILE_STATE // SCAN_LANES):
            re = pl.ds(w * SCAN_LANES, SCAN_LANES)
            im = pl.ds(SSM_TILE_STATE + w * SCAN_LANES, SCAN_LANES)
            a_re = jnp.broadcast_to(a_ref[0:1, re], (batch, SCAN_LANES))
            a_im = jnp.broadcast_to(a_ref[1:2, re], (batch, SCAN_LANES))
            h_re, h_im = h_scr[:, re] * keep, h_scr[:, im] * keep
            for t in range(n_steps):
                rows = pl.ds(t * batch, batch)
                h_re, h_im = (a_re * h_re - a_im * h_im + st_s[rows, re],
                              a_re * h_im + a_im * h_re + st_s[rows, im])
                st_s[rows, re] = h_re
                st_s[rows, im] = h_im
            h_scr[:, re] = h_re
            h_scr[:, im] = h_im

        for r in range(SSM_ROW_SPLITS):
            rs = slice(r * half, (r + 1) * half)
            y = jnp.dot(st_c[rs, :].astype(BF16), ct_ref[...], preferred_element_type=F32)
            for k in range(n_slabs):
                il_out_scr[k, rs, :] = y[:, k * LANES:(k + 1) * LANES]
        for b in range(batch):
            for k in range(n_slabs):
                y_ref[b, :, k * LANES:(k + 1) * LANES] = il_out_scr[
                    k, pl.ds(b, n_steps, stride=batch), :].astype(BF16)

    for res in range(SSM_STAGES):
        pl.when(f % SSM_STAGES == res)(functools.partial(
            stages, st_scrs[res], st_scrs[(res + 2) % SSM_STAGES], st_scrs[(res + 1) % SSM_STAGES]))


def _l1_ssm(u, bt, ct, a_tiles):
    batch, s, e = u.shape
    n_tiles = e // SSM_TILE_CH
    steps = _pick(s, SSM_CHUNK_STEPS)
    rows = steps * batch
    n_chunks = s // steps
    n_items = n_tiles * n_chunks
    blk_shape = (batch, steps, SSM_TILE_CH)
    il_scratch = pltpu.VMEM((SSM_TILE_CH // LANES, rows, LANES), F32)
    st_scratch = pltpu.VMEM((rows, 2 * SSM_TILE_STATE), F32)

    def item(f, lag):
        return jnp.clip(f - lag, 0, n_items - 1)

    def act_map(lag):
        return lambda f: (0, item(f, lag) % n_chunks, item(f, lag) // n_chunks)

    def tile_map(lag):
        return lambda f: (item(f, lag) // n_chunks, 0, 0)

    return pl.pallas_call(
        functools.partial(_ssm_kernel, n_chunks=n_chunks),
        out_shape=jax.ShapeDtypeStruct((batch, s, e), BF16),
        grid=(n_items + SSM_STAGES - 1,),
        in_specs=[
            pl.BlockSpec(blk_shape, act_map(0)),
            pl.BlockSpec((None, SSM_TILE_CH, 2 * SSM_TILE_STATE), tile_map(0)),
            pl.BlockSpec((None, 2, SSM_TILE_STATE), tile_map(1)),
            pl.BlockSpec((None, 2 * SSM_TILE_STATE, SSM_TILE_CH), tile_map(2)),
        ],
        out_specs=pl.BlockSpec(blk_shape, act_map(SSM_STAGES - 1)),
        scratch_shapes=[il_scratch, il_scratch, st_scratch, st_scratch, st_scratch,
                        pltpu.VMEM((batch, 2 * SSM_TILE_STATE), F32)],
        compiler_params=_params("arbitrary"),
        name="l1_ssm",
    )(u, bt, a_tiles, ct)


def _glu_out_kernel(y_ref, u_ref, z_ref, x1_ref, d_ref, wg_ref, bg_ref, wo_ref, o_ref):
    rows = y_ref.shape[0] // GLU_ROW_SPLITS
    for r in range(GLU_ROW_SPLITS):
        rs = slice(r * rows, (r + 1) * rows)
        y = y_ref[rs, :].astype(F32) + d_ref[...] * u_ref[rs, :].astype(F32)
        cdf = 0.5 * (1.0 + jnp.tanh(math.sqrt(2.0 / math.pi) * (y + 0.044715 * (y * y * y))))
        y = y * cdf
        gate = jnp.dot(y.astype(BF16), wg_ref[...], preferred_element_type=F32) + bg_ref[...]
        y = y * (1.0 / (1.0 + jnp.exp(-gate)))
        z = z_ref[rs, :].astype(F32)
        y = y * (z / (1.0 + jnp.exp(-z)))
        o_ref[rs, :] = x1_ref[rs, :] + jnp.dot(y.astype(BF16), wo_ref[...], preferred_element_type=F32)


def _l1_glu_out(y, u, z, x1, d_skip, wg_bf, b_glu, wo_bf):
    b, s, d = x1.shape
    e = wg_bf.shape[0]
    ts = _pick(s, GLU_ROWS)
    act_spec = pl.BlockSpec((None, ts, e), lambda bi, si: (bi, si, 0))
    return pl.pallas_call(
        _glu_out_kernel,
        out_shape=jax.ShapeDtypeStruct((b, s, d), F32),
        grid=(b, s // ts),
        in_specs=[act_spec, act_spec, act_spec,
                  pl.BlockSpec((None, ts, d), lambda bi, si: (bi, si, 0)),
                  _const_spec((1, e)), _const_spec((e, e)), _const_spec((1, e)),
                  _const_spec((e, d))],
        out_specs=pl.BlockSpec((None, ts, d), lambda bi, si: (bi, si, 0)),
        compiler_params=_params("parallel", "parallel"),
        name="l1_glu_out",
    )(y, u, z, x1, d_skip.reshape(1, e), wg_bf, b_glu.reshape(1, e), wo_bf)


def _s5_tiles(lam_re, lam_im, log_dt, b_re, b_im, c_re, c_im):
    g, p = lam_re.shape
    ch = b_re.shape[2]
    nt = g // SSM_TILE_GROUPS
    dt = jnp.exp(log_dt.astype(F32))[:, None]
    lr, li = lam_re.astype(F32), lam_im.astype(F32)
    mag = jnp.exp(lr * dt)
    ab_re, ab_im = mag * jnp.cos(li * dt), mag * jnp.sin(li * dt)
    den = lr * lr + li * li
    nr, ni = ab_re - 1.0, ab_im
    g_re = (nr * lr + ni * li) / den
    g_im = (ni * lr - nr * li) / den
    br, bi = b_re.astype(F32), b_im.astype(F32)
    bb_re = g_re[..., None] * br - g_im[..., None] * bi
    bb_im = g_re[..., None] * bi + g_im[..., None] * br
    eye = jnp.eye(SSM_TILE_GROUPS, dtype=BF16)

    def b_tile(bb):
        t = bb.astype(BF16).reshape(nt, SSM_TILE_GROUPS, p, ch)
        t = jnp.einsum("ngpc,gh->ngchp", t, eye)
        return t.reshape(nt, SSM_TILE_GROUPS * ch, SSM_TILE_GROUPS * p)

    def c_tile(cc):
        t = cc.astype(BF16).reshape(nt, SSM_TILE_GROUPS, ch, p)
        t = jnp.einsum("ngcp,gh->ngphc", t, eye)
        return t.reshape(nt, SSM_TILE_GROUPS * p, SSM_TILE_GROUPS * ch)

    bt = jnp.concatenate([b_tile(bb_re), b_tile(bb_im)], axis=2)
    ct = jnp.concatenate([c_tile(c_re.astype(F32)), c_tile(-c_im.astype(F32))], axis=1)
    a_tiles = jnp.stack([ab_re.reshape(nt, SSM_TILE_STATE), ab_im.reshape(nt, SSM_TILE_STATE)], axis=1)
    return bt, ct, a_tiles


def kernel(x, l0_norm_g, l0_w_in, l0_q_norm_g, l0_k_norm_g, l0_lam_q1, l0_lam_k1, l0_lam_q2, l0_lam_k2, l0_head_norm_g, l0_w_out, l1_norm_g, l1_w_in, l1_lam_re, l1_lam_im, l1_log_dt, l1_b_re, l1_b_im, l1_c_re, l1_c_im, l1_d, l1_w_glu, l1_b_glu, l1_w_out):
    b, s, d = x.shape
    e = l0_w_out.shape[0]
    n_heads = e // HEAD_V
    assert s % LANES == 0 and e % MXU_DIM == 0 and b == SUBLANES

    gq = l0_q_norm_g.astype(F32) * (SUB_HEAD ** -0.5 * LOG2E)
    gk = l0_k_norm_g.astype(F32)
    reps = e // SUB_HEAD
    col_scale = jnp.concatenate([jnp.tile(gq, reps), jnp.tile(gk, reps)]).reshape(1, 2 * e)
    qk, vz = _l0_inproj(x.reshape(b * s, d), s, l0_norm_g.astype(F32), l0_w_in.astype(BF16), col_scale)

    score_bound = (BF16_ROUNDING_SLACK * SUB_HEAD * jnp.max(jnp.abs(gq)) * jnp.max(jnp.abs(gk))).reshape(1)
    lam_vecs = jnp.stack([l0_lam_q1, l0_lam_k1, l0_lam_q2, l0_lam_k2]).astype(F32)
    o = _l0_attn(qk, vz, lam_vecs, l0_head_norm_g.astype(F32), score_bound, n_heads)

    x1, u, z = _l0_out_l1_in(o, x, l0_w_out.astype(BF16), l1_norm_g.astype(F32),
                             l1_w_in.astype(BF16))

    bt, ct, a_tiles = _s5_tiles(l1_lam_re, l1_lam_im, l1_log_dt, l1_b_re, l1_b_im, l1_c_re, l1_c_im)
    y = _l1_ssm(u, bt, ct, a_tiles)

    return _l1_glu_out(y, u, z, x1, l1_d.astype(F32), l1_w_glu.astype(BF16),
                       l1_b_glu.astype(F32), l1_w_out.astype(BF16))
```

```python
import functools
import math

import jax
import jax.numpy as jnp
import numpy as np
from jax import lax
from jax.experimental import pallas as pl
from jax.experimental.pallas import tpu as pltpu

F32 = jnp.float32
BF16 = jnp.bfloat16

EPS = 1e-6
SUB_HEAD = 64
HEAD_V = 2 * SUB_HEAD
GROUP_CH = 16
GROUP_STATE = 64
LAMBDA_INIT_L0 = 0.8 - 0.6 * math.exp(-0.3 * 0)
LOG2E = math.log2(math.e)

LANES = 128
SUBLANES = 8
MXU_DIM = 256
VMEM_LIMIT_BYTES = 56 * 1024 * 1024
NEG_BIG = -1e30
F32_MIN_EXP = 126
SUM_HEADROOM_BITS = 26

INPROJ_TM = 1024
INPROJ_TN = 1024
OUT_IN_ROWS = 512
SSM_CHUNK_STEPS = 128
ATTN_TQ = 512
ATTN_TQ_ONLINE = 256
FIXED_SHIFT_MAX_LOG2 = (F32_MIN_EXP - SUM_HEADROOM_BITS) / 2.0
BF16_ROUNDING_SLACK = 1.01

SSM_TILE_CH = MXU_DIM
SSM_TILE_GROUPS = SSM_TILE_CH // GROUP_CH
SSM_TILE_STATE = SSM_TILE_GROUPS * GROUP_STATE
SCAN_LANES = 256
GLU_ROWS = 512
GLU_ROW_SPLITS = 2
SSM_STAGES = 3
SSM_ROW_SPLITS = 2


def _params(*sem):
    return pltpu.CompilerParams(dimension_semantics=sem, vmem_limit_bytes=VMEM_LIMIT_BYTES)


def _const_spec(shape):
    nd = len(shape)
    return pl.BlockSpec(shape, lambda *_: (0,) * nd, pipeline_mode=pl.Buffered(1))


def _small_const_spec(shape):
    nd = len(shape)
    return pl.BlockSpec(shape, lambda *_: (0,) * nd)


def _pick(n, pref):
    t = min(n, pref)
    while n % t:
        t //= 2
    return t


def _l0_inproj_kernel(x_ref, g_ref, wqk_ref, wvz_ref, cs_ref, e_ref, qk_ref, vz_ref):
    xf = x_ref[...]
    ms = jnp.mean(xf * xf, axis=-1, keepdims=True)
    hn = (xf * lax.rsqrt(ms + EPS) * g_ref[...]).astype(BF16)
    acc = jnp.dot(hn, wqk_ref[...], preferred_element_type=F32)
    sq = acc * acc
    hi = sq.astype(BF16)
    lo = (sq - hi.astype(F32)).astype(BF16)
    e = e_ref[...]
    for c in range(acc.shape[1] // MXU_DIM):
        sl = slice(c * MXU_DIM, (c + 1) * MXU_DIM)
        ss = (jnp.dot(hi[:, sl], e, preferred_element_type=F32)
              + jnp.dot(lo[:, sl], e, preferred_element_type=F32))
        inv = lax.rsqrt(ss * (1.0 / SUB_HEAD) + EPS)
        _store_heads(qk_ref, c * (MXU_DIM // HEAD_V), (acc[:, sl] * inv * cs_ref[:, sl]).astype(BF16))
    _store_heads(vz_ref, 0, jnp.dot(hn, wvz_ref[...], preferred_element_type=F32).astype(BF16))


def _store_heads(ref, first, val):
    for h in range(val.shape[1] // HEAD_V):
        ref[first + h] = val[:, h * HEAD_V:(h + 1) * HEAD_V]


def _l0_inproj(x2d, seq, norm_g, w_bf, col_scale):
    m, d = x2d.shape
    n_half = w_bf.shape[1] // 2
    tm = _pick(seq, INPROJ_TM)
    tn = _pick(n_half, INPROJ_TN)
    n_col_tiles = n_half // tn
    seg = np.arange(MXU_DIM) // SUB_HEAD
    e = jnp.asarray(seg[:, None] == seg[None, :], dtype=BF16)
    assert seq % tm == 0 and tn % HEAD_V == 0
    n_si = seq // tm
    out = jax.ShapeDtypeStruct((m // seq, n_half // HEAD_V, seq, HEAD_V), BF16)
    out_spec = pl.BlockSpec((None, tn // HEAD_V, tm, HEAD_V), lambda i, j: (i // n_si, j, i % n_si, 0))
    return pl.pallas_call(
        _l0_inproj_kernel,
        out_shape=(out, out),
        grid=(m // tm, n_col_tiles),
        in_specs=[
            pl.BlockSpec((tm, d), lambda i, j: (i, 0)),
            _small_const_spec((1, d)),
            pl.BlockSpec((d, tn), lambda i, j: (0, j)),
            pl.BlockSpec((d, tn), lambda i, j: (0, j + n_col_tiles)),
            pl.BlockSpec((1, tn), lambda i, j: (0, j)),
            _small_const_spec((MXU_DIM, MXU_DIM)),
        ],
        out_specs=(out_spec, out_spec),
        compiler_params=_params("parallel", "parallel"),
        name="l0_inproj",
    )(x2d, norm_g.reshape(1, d), w_bf, w_bf, col_scale, e)


def _split3(val):
    p1 = val.astype(BF16).astype(F32)
    r1 = val - p1
    p2 = r1.astype(BF16).astype(F32)
    p3 = (r1 - p2).astype(BF16).astype(F32)
    return p1, p2, p3


def _augment(parts, lane, off, ones_first):
    p_off = off + 3 if ones_first else off
    o_off = off if ones_first else off + 3
    a = jnp.where(lane == p_off, parts[0], 0.0)
    a = jnp.where(lane == p_off + 1, parts[1], a)
    a = jnp.where(lane == p_off + 2, parts[2], a)
    for t in range(3):
        a = jnp.where(lane == o_off + t, 1.0, a)
    return a


def _scores(qa, kk):
    return lax.dot_general(qa, kk, (((1,), (1,)), ((), ())), preferred_element_type=F32)


def _attn_kernel(shift_ref, q_ref, k_ref, v_ref, z_ref, lam_ref, hg_ref, slope_ref, o_ref,
                 aq_scr, ak_scr, qaug_scr, kaug_scr, vaug_scr, acc_scr, acc1_scr, m_scr, l_scr,
                 *, tq, tq1):
    s_len = q_ref.shape[0]
    n_blk = s_len // tq
    half = tq // 2
    h = pl.program_id(0)
    shift = shift_ref[0]
    slope = slope_ref[pl.ds(h, 1), :] * LOG2E
    lane = lax.broadcasted_iota(jnp.int32, (tq, LANES), 1)
    row = lax.broadcasted_iota(jnp.int32, (tq, LANES), 0)

    lam_v = lam_ref[...]
    lam = (jnp.exp(jnp.sum(lam_v[0:1] * lam_v[1:2], axis=1, keepdims=True))
           - jnp.exp(jnp.sum(lam_v[2:3] * lam_v[3:4], axis=1, keepdims=True)) + LAMBDA_INIT_L0)

    def blk(i, size):
        return pl.ds(pl.multiple_of(i * size, size), size)

    def finish(rows, o0, l0, o1, l1):
        o = o0 * (1.0 / l0) - lam * (o1 * (1.0 / l1))
        ms = jnp.mean(o * o, axis=-1, keepdims=True)
        on = o * lax.rsqrt(ms + EPS) * hg_ref[...] * (1.0 - LAMBDA_INIT_L0)
        z = z_ref[rows, :].astype(F32)
        o_ref[rows, :] = (on * (z / (1.0 + jnp.exp(-z)))).astype(BF16)

    @pl.when(pl.program_id(1) == 0)
    def _tables():
        ones_col = jnp.ones((tq, HEAD_V), BF16)

        def body(c, carry):
            rows = blk(c, tq)
            pos = slope * (c * tq + row).astype(F32)
            k_parts = _split3(pos)
            q_parts = _split3(-(pos + shift))
            ak_scr[0, rows, :] = _augment(k_parts, lane, SUB_HEAD, True).astype(BF16)
            ak_scr[1, rows, :] = _augment(k_parts, lane, 0, True).astype(BF16)
            aq_scr[0, rows, :] = _augment(q_parts, lane, SUB_HEAD, False).astype(BF16)
            aq_scr[1, rows, :] = _augment(q_parts, lane, 0, False).astype(BF16)
            vaug_scr[rows, HEAD_V:2 * HEAD_V] = ones_col
            return carry

        lax.fori_loop(0, n_blk, body, 0)

    lane_row = lax.broadcasted_iota(jnp.int32, (1, LANES), 1)
    keep = (jnp.where(lane_row < SUB_HEAD, 1.0, 0.0).astype(BF16),
            jnp.where(lane_row < SUB_HEAD, 0.0, 1.0).astype(BF16))

    def build(rows):
        kb = k_ref[rows, :]
        qb = q_ref[rows, :]
        for sub in range(2):
            kaug_scr[sub, rows, :] = kb * keep[sub] + ak_scr[sub, rows, :]
            qaug_scr[sub, rows, :] = qb * keep[sub] + aq_scr[sub, rows, :]
        vaug_scr[rows, 0:HEAD_V] = v_ref[rows, :]

    fixed_shift_ok = shift <= FIXED_SHIFT_MAX_LOG2

    @pl.when(fixed_shift_ok)
    def _fixed_shift():
        def tile(q_rows, k_rows, mask_off, init):
            n_q, n_k = q_rows.size, k_rows.size
            for sub in range(2):
                s = _scores(qaug_scr[sub, q_rows, :], kaug_scr[sub, k_rows, :])
                if mask_off is not None:
                    cols = lax.broadcasted_iota(jnp.int32, (n_q, n_k), 1)
                    rws = lax.broadcasted_iota(jnp.int32, (n_q, n_k), 0)
                    s = jnp.where(cols <= rws + mask_off, s, -jnp.inf)
                pv = jnp.dot(jnp.exp2(s).astype(BF16), vaug_scr[k_rows, :], preferred_element_type=F32)
                if init:
                    acc_scr[sub, q_rows, :] = pv
                else:
                    acc_scr[sub, q_rows, :] += pv

        for i in range(n_blk):
            r0 = i * tq
            build(pl.ds(r0, tq))
            tile(pl.ds(r0, half), pl.ds(r0, half), 0, True)
            tile(pl.ds(r0 + half, half), pl.ds(r0, tq), half, True)
            for j in range(i):
                tile(pl.ds(r0, tq), pl.ds(j * tq, tq), None, False)
            rows = pl.ds(r0, tq)
            a0 = acc_scr[0, rows, :]
            a1 = acc_scr[1, rows, :]
            finish(rows, a0[:, :HEAD_V], a0[:, HEAD_V:], a1[:, :HEAD_V], a1[:, HEAD_V:])

    @pl.when(jnp.logical_not(fixed_shift_ok))
    def _online():
        col_s = lax.broadcasted_iota(jnp.int32, (tq1, tq1), 1)
        row_s = lax.broadcasted_iota(jnp.int32, (tq1, tq1), 0)
        causal = col_s <= row_s

        def build_block(c, carry):
            build(blk(c, tq))
            return carry

        lax.fori_loop(0, n_blk, build_block, 0)

        def q_block(i, carry):
            rows = blk(i, tq1)
            m_scr[...] = jnp.full(m_scr.shape, NEG_BIG, F32)
            l_scr[...] = jnp.zeros(l_scr.shape, F32)
            acc1_scr[...] = jnp.zeros(acc1_scr.shape, F32)

            def kv_step(j, masked):
                k_rows = blk(j, tq1)
                v = v_ref[k_rows, :]
                for sub in range(2):
                    s = _scores(qaug_scr[sub, rows, :], kaug_scr[sub, k_rows, :])
                    if masked:
                        s = jnp.where(causal, s, -jnp.inf)
                    m_prev = m_scr[sub]
                    m_new = jnp.maximum(m_prev, jnp.max(s, axis=1, keepdims=True))
                    alpha = jnp.exp2(m_prev - m_new)
                    p = jnp.exp2(s - m_new)
                    l_scr[sub] = alpha * l_scr[sub] + jnp.sum(p, axis=1, keepdims=True)
                    acc1_scr[sub] = alpha * acc1_scr[sub] + jnp.dot(
                        p.astype(BF16), v, preferred_element_type=F32)
                    m_scr[sub] = m_new

            def off_diag(j, c):
                kv_step(j, False)
                return c

            lax.fori_loop(0, i, off_diag, 0)
            kv_step(i, True)
            finish(rows, acc1_scr[0], l_scr[0], acc1_scr[1], l_scr[1])
            return carry

        lax.fori_loop(0, s_len // tq1, q_block, 0)


def _l0_attn(qk, vz, lam_vecs, head_g, score_bound, n_heads):
    b, _, s, _ = qk.shape
    tq = _pick(s, ATTN_TQ)
    tq1 = _pick(s, ATTN_TQ_ONLINE)
    slopes = 2.0 ** (-8.0 * np.arange(1, n_heads + 1) / n_heads)
    slope_tab = jnp.asarray(np.broadcast_to(slopes[:, None], (n_heads, LANES)), dtype=F32)

    def head_block(which):
        return pl.BlockSpec((None, None, s, HEAD_V), lambda hi, bi: (bi, which * n_heads + hi, 0, 0))

    smem = pl.BlockSpec(memory_space=pltpu.SMEM)
    return pl.pallas_call(
        functools.partial(_attn_kernel, tq=tq, tq1=tq1),
        out_shape=jax.ShapeDtypeStruct((b, n_heads, s, HEAD_V), BF16),
        grid=(n_heads, b),
        in_specs=[smem, head_block(0), head_block(1), head_block(0), head_block(1),
                  _small_const_spec((4, SUB_HEAD)), _small_const_spec((1, HEAD_V)),
                  _small_const_spec((n_heads, LANES))],
        out_specs=pl.BlockSpec((None, None, s, HEAD_V), lambda hi, bi: (bi, hi, 0, 0)),
        scratch_shapes=[pltpu.VMEM((2, s, HEAD_V), BF16),
                        pltpu.VMEM((2, s, HEAD_V), BF16),
                        pltpu.VMEM((2, s, HEAD_V), BF16),
                        pltpu.VMEM((2, s, HEAD_V), BF16),
                        pltpu.VMEM((s, 2 * HEAD_V), BF16),
                        pltpu.VMEM((2, s, 2 * HEAD_V), F32),
                        pltpu.VMEM((2, tq1, HEAD_V), F32),
                        pltpu.VMEM((2, tq1, 1), F32),
                        pltpu.VMEM((2, tq1, 1), F32)],
        compiler_params=_params("parallel", "arbitrary"),
        name="l0_attn",
    )(score_bound, qk, qk, vz, vz, lam_vecs, head_g.reshape(1, HEAD_V), slope_tab)


def _l0_out_l1_in_kernel(o_ref, x_ref, wo_ref, g_ref, wi_ref, x1_ref, u_ref, z_ref):
    e = u_ref.shape[1]
    o = jnp.concatenate([o_ref[h] for h in range(o_ref.shape[0])], axis=1)
    x1 = x_ref[...] + jnp.dot(o, wo_ref[...], preferred_element_type=F32)
    x1_ref[...] = x1
    ms = jnp.mean(x1 * x1, axis=-1, keepdims=True)
    hn = (x1 * lax.rsqrt(ms + EPS) * g_ref[...]).astype(BF16)
    u_ref[...] = jnp.dot(hn, wi_ref[:, :e], preferred_element_type=F32).astype(BF16)
    z_ref[...] = jnp.dot(hn, wi_ref[:, e:], preferred_element_type=F32).astype(BF16)


def _l0_out_l1_in(o, x, wo_bf, norm_g, wi_bf):
    b, s, d = x.shape
    n_heads = o.shape[1]
    e = n_heads * HEAD_V
    tm = _pick(s, OUT_IN_ROWS)
    act_shape = jax.ShapeDtypeStruct((b, s, e), BF16)
    return pl.pallas_call(
        _l0_out_l1_in_kernel,
        out_shape=(jax.ShapeDtypeStruct((b, s, d), F32), act_shape, act_shape),
        grid=(b, s // tm),
        in_specs=[
            pl.BlockSpec((None, n_heads, tm, HEAD_V), lambda bi, si: (bi, 0, si, 0)),
            pl.BlockSpec((None, tm, d), lambda bi, si: (bi, si, 0)),
            _const_spec((e, d)),
            _small_const_spec((1, d)),
            _const_spec((d, 2 * e)),
        ],
        out_specs=(
            pl.BlockSpec((None, tm, d), lambda bi, si: (bi, si, 0)),
            pl.BlockSpec((None, tm, e), lambda bi, si: (bi, si, 0)),
            pl.BlockSpec((None, tm, e), lambda bi, si: (bi, si, 0)),
        ),
        compiler_params=_params("parallel", "parallel"),
        name="l0_out_l1_in",
    )(o, x, wo_bf, norm_g.reshape(1, d), wi_bf)


def _ssm_kernel(u_ref, bt_ref, a_ref, ct_ref, y_ref, il_in_scr, il_out_scr, st0_scr, st1_scr, st2_scr, h_scr,
                *, n_chunks):
    batch, n_steps, _ = u_ref.shape
    n_slabs = il_in_scr.shape[0]
    half = batch * n_steps // SSM_ROW_SPLITS
    f = pl.program_id(0)
    st_scrs = (st0_scr, st1_scr, st2_scr)

    @pl.when(f == 0)
    def _():
        for st in st_scrs:
            st[...] = jnp.zeros(st.shape, F32)
        h_scr[...] = jnp.zeros(h_scr.shape, F32)

    keep = jnp.where((f - 1) % n_chunks == 0, 0.0, 1.0).astype(F32)

    def stages(st_b, st_s, st_c):
        for b in range(batch):
            ub = u_ref[b].astype(F32)
            for k in range(n_slabs):
                il_in_scr[k, pl.ds(b, n_steps, stride=batch), :] = ub[:, k * LANES:(k + 1) * LANES]
        for r in range(SSM_ROW_SPLITS):
            rs = slice(r * half, (r + 1) * half)
            u_il = jnp.concatenate([il_in_scr[k, rs, :] for k in range(n_slabs)], axis=1).astype(BF16)
            st_b[rs, :] = jnp.dot(u_il, bt_ref[...], preferred_element_type=F32)

        for w in range(SSM_TILE_STATE // SCAN_LANES):
            re = pl.ds(w * SCAN_LANES, SCAN_LANES)
            im = pl.ds(SSM_TILE_STATE + w * SCAN_LANES, SCAN_LANES)
            a_re = jnp.broadcast_to(a_ref[0:1, re], (batch, SCAN_LANES))
            a_im = jnp.broadcast_to(a_ref[1:2, re], (batch, SCAN_LANES))
            h_re, h_im = h_scr[:, re] * keep, h_scr[:, im] * keep
            for t in range(n_steps):
                rows = pl.ds(t * batch, batch)
                h_re, h_im = (a_re * h_re - a_im * h_im + st_s[rows, re],
                              a_re * h_im + a_im * h_re + st_s[rows, im])
                st_s[rows, re] = h_re
                st_s[rows, im] = h_im
            h_scr[:, re] = h_re
            h_scr[:, im] = h_im

        for r in range(SSM_ROW_SPLITS):
            rs = slice(r * half, (r + 1) * half)
            y = jnp.dot(st_c[rs, :].astype(BF16), ct_ref[...], preferred_element_type=F32)
            for k in range(n_slabs):
                il_out_scr[k, rs, :] = y[:, k * LANES:(k + 1) * LANES]
        for b in range(batch):
            for k in range(n_slabs):
                y_ref[b, :, k * LANES:(k + 1) * LANES] = il_out_scr[
                    k, pl.ds(b, n_steps, stride=batch), :].astype(BF16)

    for res in range(SSM_STAGES):
        pl.when(f % SSM_STAGES == res)(functools.partial(
            stages, st_scrs[res], st_scrs[(res + 2) % SSM_STAGES], st_scrs[(res + 1) % SSM_STAGES]))


def _l1_ssm(u, bt, ct, a_tiles):
    batch, s, e = u.shape
    n_tiles = e // SSM_TILE_CH
    steps = _pick(s, SSM_CHUNK_STEPS)
    rows = steps * batch
    n_chunks = s // steps
    n_items = n_tiles * n_chunks
    blk_shape = (batch, steps, SSM_TILE_CH)
    il_scratch = pltpu.VMEM((SSM_TILE_CH // LANES, rows, LANES), F32)
    st_scratch = pltpu.VMEM((rows, 2 * SSM_TILE_STATE), F32)

    def item(f, lag):
        return jnp.clip(f - lag, 0, n_items - 1)

    def act_map(lag):
        return lambda f: (0, item(f, lag) % n_chunks, item(f, lag) // n_chunks)

    def tile_map(lag):
        return lambda f: (item(f, lag) // n_chunks, 0, 0)

    return pl.pallas_call(
        functools.partial(_ssm_kernel, n_chunks=n_chunks),
        out_shape=jax.ShapeDtypeStruct((batch, s, e), BF16),
        grid=(n_items + SSM_STAGES - 1,),
        in_specs=[
            pl.BlockSpec(blk_shape, act_map(0)),
            pl.BlockSpec((None, SSM_TILE_CH, 2 * SSM_TILE_STATE), tile_map(0)),
            pl.BlockSpec((None, 2, SSM_TILE_STATE), tile_map(1)),
            pl.BlockSpec((None, 2 * SSM_TILE_STATE, SSM_TILE_CH), tile_map(2)),
        ],
        out_specs=pl.BlockSpec(blk_shape, act_map(SSM_STAGES - 1)),
        scratch_shapes=[il_scratch, il_scratch, st_scratch, st_scratch, st_scratch,
                        pltpu.VMEM((batch, 2 * SSM_TILE_STATE), F32)],
        compiler_params=_params("arbitrary"),
        name="l1_ssm",
    )(u, bt, a_tiles, ct)


def _glu_out_kernel(y_ref, u_ref, z_ref, x1_ref, d_ref, wg_ref, bg_ref, wo_ref, o_ref):
    rows = y_ref.shape[0] // GLU_ROW_SPLITS
    for r in range(GLU_ROW_SPLITS):
        rs = slice(r * rows, (r + 1) * rows)
        y = y_ref[rs, :].astype(F32) + d_ref[...] * u_ref[rs, :].astype(F32)
        cdf = 0.5 * (1.0 + jnp.tanh(math.sqrt(2.0 / math.pi) * (y + 0.044715 * (y * y * y))))
        y = y * cdf
        gate = jnp.dot(y.astype(BF16), wg_ref[...], preferred_element_type=F32) + bg_ref[...]
        y = y * (1.0 / (1.0 + jnp.exp(-gate)))
        z = z_ref[rs, :].astype(F32)
        y = y * (z / (1.0 + jnp.exp(-z)))
        o_ref[rs, :] = x1_ref[rs, :] + jnp.dot(y.astype(BF16), wo_ref[...], preferred_element_type=F32)


def _l1_glu_out(y, u, z, x1, d_skip, wg_bf, b_glu, wo_bf):
    b, s, d = x1.shape
    e = wg_bf.shape[0]
    ts = _pick(s, GLU_ROWS)
    act_spec = pl.BlockSpec((None, ts, e), lambda bi, si: (bi, si, 0))
    return pl.pallas_call(
        _glu_out_kernel,
        out_shape=jax.ShapeDtypeStruct((b, s, d), F32),
        grid=(b, s // ts),
        in_specs=[act_spec, act_spec, act_spec,
                  pl.BlockSpec((None, ts, d), lambda bi, si: (bi, si, 0)),
                  _small_const_spec((1, e)), _const_spec((e, e)), _small_const_spec((1, e)),
                  _const_spec((e, d))],
        out_specs=pl.BlockSpec((None, ts, d), lambda bi, si: (bi, si, 0)),
        compiler_params=_params("parallel", "parallel"),
        name="l1_glu_out",
    )(y, u, z, x1, d_skip.reshape(1, e), wg_bf, b_glu.reshape(1, e), wo_bf)


def _s5_tiles(lam_re, lam_im, log_dt, b_re, b_im, c_re, c_im):
    g, p = lam_re.shape
    ch = b_re.shape[2]
    nt = g // SSM_TILE_GROUPS
    dt = jnp.exp(log_dt.astype(F32))[:, None]
    lr, li = lam_re.astype(F32), lam_im.astype(F32)
    mag = jnp.exp(lr * dt)
    ab_re, ab_im = mag * jnp.cos(li * dt), mag * jnp.sin(li * dt)
    den = lr * lr + li * li
    nr, ni = ab_re - 1.0, ab_im
    g_re = (nr * lr + ni * li) / den
    g_im = (ni * lr - nr * li) / den
    br, bi = b_re.astype(F32), b_im.astype(F32)
    bb_re = g_re[..., None] * br - g_im[..., None] * bi
    bb_im = g_re[..., None] * bi + g_im[..., None] * br
    eye = jnp.eye(SSM_TILE_GROUPS, dtype=BF16)

    def b_tile(bb):
        t = bb.astype(BF16).reshape(nt, SSM_TILE_GROUPS, p, ch)
        t = jnp.einsum("ngpc,gh->ngchp", t, eye)
        return t.reshape(nt, SSM_TILE_GROUPS * ch, SSM_TILE_GROUPS * p)

    def c_tile(cc):
        t = cc.astype(BF16).reshape(nt, SSM_TILE_GROUPS, ch, p)
        t = jnp.einsum("ngcp,gh->ngphc", t, eye)
        return t.reshape(nt, SSM_TILE_GROUPS * p, SSM_TILE_GROUPS * ch)

    bt = jnp.concatenate([b_tile(bb_re), b_tile(bb_im)], axis=2)
    ct = jnp.concatenate([c_tile(c_re.astype(F32)), c_tile(-c_im.astype(F32))], axis=1)
    a_tiles = jnp.stack([ab_re.reshape(nt, SSM_TILE_STATE), ab_im.reshape(nt, SSM_TILE_STATE)], axis=1)
    return bt, ct, a_tiles


def kernel(x, l0_norm_g, l0_w_in, l0_q_norm_g, l0_k_norm_g, l0_lam_q1, l0_lam_k1, l0_lam_q2, l0_lam_k2, l0_head_norm_g, l0_w_out, l1_norm_g, l1_w_in, l1_lam_re, l1_lam_im, l1_log_dt, l1_b_re, l1_b_im, l1_c_re, l1_c_im, l1_d, l1_w_glu, l1_b_glu, l1_w_out):
    b, s, d = x.shape
    e = l0_w_out.shape[0]
    n_heads = e // HEAD_V
    assert s % LANES == 0 and e % MXU_DIM == 0 and b == SUBLANES

    gq = l0_q_norm_g.astype(F32) * (SUB_HEAD ** -0.5 * LOG2E)
    gk = l0_k_norm_g.astype(F32)
    reps = e // SUB_HEAD
    col_scale = jnp.concatenate([jnp.tile(gq, reps), jnp.tile(gk, reps)]).reshape(1, 2 * e)
    qk, vz = _l0_inproj(x.reshape(b * s, d), s, l0_norm_g.astype(F32), l0_w_in.astype(BF16), col_scale)

    score_bound = (BF16_ROUNDING_SLACK * SUB_HEAD * jnp.max(jnp.abs(gq)) * jnp.max(jnp.abs(gk))).reshape(1)
    lam_vecs = jnp.stack([l0_lam_q1, l0_lam_k1, l0_lam_q2, l0_lam_k2]).astype(F32)
    o = _l0_attn(qk, vz, lam_vecs, l0_head_norm_g.astype(F32), score_bound, n_heads)

    x1, u, z = _l0_out_l1_in(o, x, l0_w_out.astype(BF16), l1_norm_g.astype(F32),
                             l1_w_in.astype(BF16))

    bt, ct, a_tiles = _s5_tiles(l1_lam_re, l1_lam_im, l1_log_dt, l1_b_re, l1_b_im, l1_c_re, l1_c_im)
    y = _l1_ssm(u, bt, ct, a_tiles)

    return _l1_glu_out(y, u, z, x1, l1_d.astype(F32), l1_w_glu.astype(BF16),
                       l1_b_glu.astype(F32), l1_w_out.astype(BF16))
```
